```python
import jax, jax.numpy as jnp
from jax import lax
import numpy as np

D_MODEL = 4096
BATCH = 4
SEQ = 2048
DEPTH = 2
DEC_BATCH = 8
DEC_SEQ = 8
PAST_LEN = 16384
PAGE_SIZE = 128

GLA_HEADS = 8
GLA_DK = D_MODEL // 32
GLA_DV = D_MODEL // 16
GLA_KEY_WIDTH = GLA_HEADS * GLA_DK
GLA_VALUE_WIDTH = GLA_HEADS * GLA_DV
GLA_GATE_RANK = 16
GLA_TAU = 16.0
GLA_CHUNK = 64
HEAD_DIM = 128
WINDOWS = (128, 512, 2048)
DILATIONS = (1, 4, 16)
N_DIL_GROUPS = 3
HEADS_PER_GROUP = D_MODEL // 512
ATT_HEADS = N_DIL_GROUPS * HEADS_PER_GROUP
ATT_WIDTH = ATT_HEADS * HEAD_DIM
ATT_OUT_WIDTH = HEADS_PER_GROUP * HEAD_DIM
Q_BLOCK = 128
N_GROUPS = 4
EXPERTS_PER_GROUP = 8
N_EXPERTS = N_GROUPS * EXPERTS_PER_GROUP
TOP_K = 2
D_EXPERT = D_MODEL // 4
MOE_BLOCK = 128
N_IN = 2 * GLA_KEY_WIDTH + 2 * GLA_VALUE_WIDTH + GLA_GATE_RANK + 3 * ATT_WIDTH
EPS = 1e-6

kernel_name = 'hybrid_gla_dilated_hmoe_decode_step'


def rms_norm(x, g):
    xf = x.astype(jnp.float32)
    y = xf * lax.rsqrt(jnp.mean(xf * xf, axis=-1, keepdims=True) + EPS)
    return (y * g.astype(jnp.float32)).astype(x.dtype)


def split_columns(p):
    sizes = (GLA_KEY_WIDTH, GLA_KEY_WIDTH, GLA_VALUE_WIDTH, GLA_VALUE_WIDTH, GLA_GATE_RANK,
             ATT_WIDTH, ATT_WIDTH, ATT_WIDTH)
    return jnp.split(p, np.cumsum(sizes)[:-1].tolist(), axis=-1)


def project_inputs(h, w_in, w_gla_gate_up, b_gla_gate, g_q, g_k):
    B, T, _ = h.shape
    gq, gk, gv, gr, glr, aq, ak, av = split_columns(h @ w_in)
    gla_q = gq.reshape(B, T, GLA_HEADS, GLA_DK) * (GLA_DK ** -0.5)
    gla_k = gk.reshape(B, T, GLA_HEADS, GLA_DK)
    gla_v = gv.reshape(B, T, GLA_HEADS, GLA_DV)
    z = (glr @ w_gla_gate_up + b_gla_gate).astype(jnp.float32)
    log_a = (jax.nn.log_sigmoid(z) / GLA_TAU).reshape(B, T, GLA_HEADS, GLA_DK)
    aq = rms_norm(aq.reshape(B, T, ATT_HEADS, HEAD_DIM), g_q) * (HEAD_DIM ** -0.5)
    ak = rms_norm(ak.reshape(B, T, ATT_HEADS, HEAD_DIM), g_k)
    av = av.reshape(B, T, ATT_HEADS, HEAD_DIM)
    return gla_q, gla_k, gla_v, log_a, gr, aq, ak, av


def gla_chunked(q, k, v, log_a, s0, chunk):
    B, T, H, DK = q.shape
    DV = v.shape[-1]
    nc = T // chunk

    def to_chunks(a):
        return a.astype(jnp.float32).reshape(B, nc, chunk, H, a.shape[-1]).transpose(1, 0, 3, 2, 4)

    causal = jnp.tril(jnp.ones((chunk, chunk), dtype=bool))[:, :, None]

    def step(state, inp):
        qc, kc, vc, ac = inp
        b = jnp.cumsum(ac, axis=2)
        o_inter = jnp.einsum('bhtk,bhkv->bhtv', qc * jnp.exp(b), state)
        diff = b[:, :, :, None, :] - b[:, :, None, :, :]
        decay = jnp.exp(jnp.where(causal, diff, -jnp.inf))
        scores = jnp.einsum('bhtk,bhsk,bhtsk->bhts', qc, kc, decay)
        o_intra = jnp.einsum('bhts,bhsv->bhtv', scores, vc)
        b_end = b[:, :, -1:, :]
        state = (jnp.exp(b_end[:, :, 0, :])[..., None] * state
                 + jnp.einsum('bhsk,bhsv->bhkv', kc * jnp.exp(b_end - b), vc))
        return state, o_inter + o_intra

    state, o = lax.scan(step, s0.astype(jnp.float32),
                        (to_chunks(q), to_chunks(k), to_chunks(v), to_chunks(log_a)))
    o = o.transpose(1, 0, 3, 2, 4).reshape(B, T, H, DV)
    return o, state


def dilated_attend(q, k_src, v_src, q_idx, dilation, n_keys):
    idx = q_idx[:, None] - dilation * jnp.arange(n_keys, dtype=jnp.int32)[None, :]
    valid = idx >= 0
    idx = jnp.maximum(idx, 0)
    kg = jnp.take(k_src, idx, axis=1)
    vg = jnp.take(v_src, idx, axis=1)
    s = jnp.einsum('bqhd,bqkhd->bqhk', q, kg).astype(jnp.float32)
    s = jnp.where(valid[None, :, None, :], s, -jnp.inf)
    m = jnp.max(s, axis=-1, keepdims=True)
    p = jnp.exp(s - m)
    den = jnp.sum(p, axis=-1, keepdims=True)
    o = jnp.einsum('bqhk,bqkhd->bqhd', (p / den).astype(v_src.dtype), vg)
    lse = (m + jnp.log(den))[..., 0]
    return o.astype(q.dtype), lse


def prompt_dilated(q, k, v, dilation, n_keys):
    B, S, H, hd = q.shape
    nb = S // Q_BLOCK
    q_blocks = q.reshape(B, nb, Q_BLOCK, H, hd).transpose(1, 0, 2, 3, 4)
    starts = jnp.arange(nb, dtype=jnp.int32) * Q_BLOCK

    def one_block(args):
        qb, s0 = args
        return dilated_attend(qb, k, v, s0 + jnp.arange(Q_BLOCK, dtype=jnp.int32), dilation, n_keys)

    o, lse = lax.map(one_block, (q_blocks, starts))
    o = o.transpose(1, 0, 2, 3, 4).reshape(B, S, H, hd)
    lse = lse.transpose(1, 0, 2, 3).reshape(B, S, H)
    return o, lse


def combine_dilations(outs, lses):
    o = jnp.stack(outs)
    wts = jax.nn.softmax(jnp.stack(lses), axis=0)
    return jnp.sum(o * wts[..., None].astype(o.dtype), axis=0)


def merge_branches(h, o_gla, gla_r, o_att, g_gla_out, w_gla_proj, w_att_proj, w_merge_gate, w_out):
    B, T, _ = h.shape
    o_gla = rms_norm(o_gla.astype(h.dtype), g_gla_out).reshape(B, T, GLA_VALUE_WIDTH) * jax.nn.silu(gla_r)
    gate_gla, gate_att = jnp.split(jax.nn.sigmoid(h @ w_merge_gate), 2, axis=-1)
    mixed = gate_gla * (o_gla @ w_gla_proj) + gate_att * (o_att.reshape(B, T, ATT_OUT_WIDTH) @ w_att_proj)
    return mixed @ w_out


def moe_ffn(h, w_rg, w_re, w_g, w_u, w_d):
    N, D = h.shape
    g_logits = (h @ w_rg).astype(jnp.float32)
    g_sel = jnp.argmax(g_logits, axis=-1)
    g_prob = jnp.take_along_axis(jax.nn.softmax(g_logits, axis=-1), g_sel[:, None], axis=-1)
    e_logits = (h @ w_re).astype(jnp.float32).reshape(N, N_GROUPS, EXPERTS_PER_GROUP)
    e_logits = jnp.take_along_axis(e_logits, g_sel[:, None, None], axis=1)[:, 0]
    top_v, top_i = lax.top_k(e_logits, TOP_K)
    gate = g_prob * jax.nn.softmax(top_v, axis=-1)
    eid = (g_sel[:, None] * EXPERTS_PER_GROUP + top_i).reshape(-1).astype(jnp.int32)
    tok = jnp.repeat(jnp.arange(N, dtype=jnp.int32), TOP_K)
    order = jnp.argsort(eid)
    eid_s, tok_s, w_s = eid[order], tok[order], gate.reshape(-1)[order]
    counts = jnp.bincount(eid, length=N_EXPERTS)
    starts = jnp.cumsum(counts) - counts
    padded = (counts + MOE_BLOCK - 1) // MOE_BLOCK * MOE_BLOCK
    pend = jnp.cumsum(padded)
    pstarts = pend - padded
    dest = pstarts[eid_s] + (jnp.arange(N * TOP_K, dtype=jnp.int32) - starts[eid_s])
    n_blocks = (N * TOP_K + N_EXPERTS * (MOE_BLOCK - 1) + MOE_BLOCK - 1) // MOE_BLOCK
    rows = n_blocks * MOE_BLOCK
    row_tok = jnp.full((rows,), N, dtype=jnp.int32).at[dest].set(tok_s)
    h_ext = jnp.concatenate([h, jnp.zeros((1, D), h.dtype)], axis=0)
    h_pad = h_ext[row_tok].reshape(n_blocks, MOE_BLOCK, D)
    blk_e = jnp.minimum(jnp.searchsorted(pend, jnp.arange(n_blocks, dtype=jnp.int32) * MOE_BLOCK, side='right'),
                        N_EXPERTS - 1)

    def expert_block(args):
        xb, e = args
        return (jax.nn.silu(xb @ w_g[e]) * (xb @ w_u[e])) @ w_d[e]

    y_pad = lax.map(expert_block, (h_pad, blk_e)).reshape(rows, D)
    contrib = y_pad[dest] * w_s[:, None].astype(h.dtype)
    return jnp.zeros_like(h).at[tok_s].add(contrib)


def setup_inputs(seed: int = 0) -> dict:
    key = jax.random.key(seed)
    ks = jax.random.split(key, 24)
    f32 = jnp.float32

    def nrm(k, shape, scale):
        return jax.random.normal(k, shape, f32) * scale

    def gain(k, shape):
        return 1.0 + 0.02 * jax.random.normal(k, shape, f32)

    return {
        'x_prompt': nrm(ks[0], (BATCH, SEQ, D_MODEL), 1.0),
        'x_sample': nrm(ks[1], (DEC_BATCH, DEC_SEQ, D_MODEL), 1.0),
        'cache_kv_w128': nrm(ks[2], (DEPTH, DEC_BATCH, min(WINDOWS[0], PAST_LEN), 2, HEADS_PER_GROUP, HEAD_DIM), 1.0),
        'cache_kv_w512': nrm(ks[3], (DEPTH, DEC_BATCH, min(WINDOWS[1], PAST_LEN), 2, HEADS_PER_GROUP, HEAD_DIM), 1.0),
        'cache_kv_w2048': nrm(ks[4], (DEPTH, DEC_BATCH, min(WINDOWS[2], PAST_LEN), 2, HEADS_PER_GROUP, HEAD_DIM), 1.0),
        'state_gla': nrm(ks[5], (DEPTH, DEC_BATCH, GLA_HEADS, GLA_DK, GLA_DV), 1.0),
        'g_norm_mix': gain(ks[6], (DEPTH, D_MODEL)),
        'w_in': nrm(ks[7], (DEPTH, D_MODEL, N_IN), D_MODEL ** -0.5),
        'w_gla_gate_up': nrm(ks[8], (DEPTH, GLA_GATE_RANK, GLA_KEY_WIDTH), GLA_GATE_RANK ** -0.5),
        'b_gla_gate': nrm(ks[9], (DEPTH, GLA_KEY_WIDTH), 0.1),
        'g_q': gain(ks[10], (DEPTH, HEAD_DIM)),
        'g_k': gain(ks[11], (DEPTH, HEAD_DIM)),
        'g_gla_out': gain(ks[12], (DEPTH, GLA_DV)),
        'w_gla_proj': nrm(ks[13], (DEPTH, GLA_VALUE_WIDTH, D_MODEL), GLA_VALUE_WIDTH ** -0.5),
        'w_att_proj': nrm(ks[14], (DEPTH, ATT_OUT_WIDTH, D_MODEL), ATT_OUT_WIDTH ** -0.5),
        'w_merge_gate': nrm(ks[15], (DEPTH, D_MODEL, 2 * D_MODEL), D_MODEL ** -0.5),
        'w_out': nrm(ks[16], (DEPTH, D_MODEL, D_MODEL), D_MODEL ** -0.5),
        'g_norm_ffn': gain(ks[17], (DEPTH, D_MODEL)),
        'w_router_group': nrm(ks[18], (DEPTH, D_MODEL, N_GROUPS), D_MODEL ** -0.5),
        'w_router_expert': nrm(ks[19], (DEPTH, D_MODEL, N_EXPERTS), D_MODEL ** -0.5),
        'w_exp_gate': nrm(ks[20], (DEPTH, N_EXPERTS, D_MODEL, D_EXPERT), D_MODEL ** -0.5),
        'w_exp_up': nrm(ks[21], (DEPTH, N_EXPERTS, D_MODEL, D_EXPERT), D_MODEL ** -0.5),
        'w_exp_down': nrm(ks[22], (DEPTH, N_EXPERTS, D_EXPERT, D_MODEL), D_EXPERT ** -0.5),
    }


def reference(x_prompt, x_sample, cache_kv_w128, cache_kv_w512, cache_kv_w2048, state_gla,
              g_norm_mix, w_in, w_gla_gate_up, b_gla_gate, g_q, g_k, g_gla_out,
              w_gla_proj, w_att_proj, w_merge_gate, w_out, g_norm_ffn,
              w_router_group, w_router_expert, w_exp_gate, w_exp_up, w_exp_down):
    caches = (cache_kv_w128, cache_kv_w512, cache_kv_w2048)
    xp, xs = x_prompt, x_sample
    B, S, D = xp.shape
    DB, DS, _ = xs.shape
    new_kv_p = [[] for _ in range(N_DIL_GROUPS)]
    new_kv_s = [[] for _ in range(N_DIL_GROUPS)]
    gla_p, gla_s = [], []
    for l in range(DEPTH):
        hp = rms_norm(xp, g_norm_mix[l])
        hs = rms_norm(xs, g_norm_mix[l])
        pp = project_inputs(hp, w_in[l], w_gla_gate_up[l], b_gla_gate[l], g_q[l], g_k[l])
        ps = project_inputs(hs, w_in[l], w_gla_gate_up[l], b_gla_gate[l], g_q[l], g_k[l])

        s0p = jnp.zeros((B, GLA_HEADS, GLA_DK, GLA_DV), jnp.float32)
        o_gla_p, st_p = gla_chunked(pp[0], pp[1], pp[2], pp[3], s0p, GLA_CHUNK)
        o_gla_s, st_s = gla_chunked(ps[0], ps[1], ps[2], ps[3], state_gla[l], DS)
        gla_p.append(st_p.astype(state_gla.dtype))
        gla_s.append(st_s.astype(state_gla.dtype))

        outs_p, lses_p, outs_s, lses_s = [], [], [], []
        for g in range(N_DIL_GROUPS):
            hsl = slice(g * HEADS_PER_GROUP, (g + 1) * HEADS_PER_GROUP)
            n_keys = WINDOWS[g] // DILATIONS[g] + 1
            qp, kp, vp = pp[5][:, :, hsl], pp[6][:, :, hsl], pp[7][:, :, hsl]
            o, lse = prompt_dilated(qp, kp, vp, DILATIONS[g], n_keys)
            outs_p.append(o)
            lses_p.append(lse)
            kv_p = jnp.stack([kp, vp], axis=2)
            new_kv_p[g].append(kv_p[:, S - min(WINDOWS[g], S):].astype(caches[g].dtype))

            qs, ks_, vs = ps[5][:, :, hsl], ps[6][:, :, hsl], ps[7][:, :, hsl]
            kv_s = jnp.stack([ks_, vs], axis=2).astype(caches[g].dtype)
            buf = jnp.concatenate([caches[g][l], kv_s], axis=1)
            L = caches[g].shape[2]
            o, lse = dilated_attend(qs, buf[:, :, 0], buf[:, :, 1],
                                    L + jnp.arange(DS, dtype=jnp.int32), DILATIONS[g], n_keys)
            outs_s.append(o)
            lses_s.append(lse)
            new_kv_s[g].append(kv_s)

        att_p = combine_dilations(outs_p, lses_p)
        att_s = combine_dilations(outs_s, lses_s)
        xp = xp + merge_branches(hp, o_gla_p, pp[4], att_p, g_gla_out[l], w_gla_proj[l], w_att_proj[l],
                                 w_merge_gate[l], w_out[l])
        xs = xs + merge_branches(hs, o_gla_s, ps[4], att_s, g_gla_out[l], w_gla_proj[l], w_att_proj[l],
                                 w_merge_gate[l], w_out[l])

        h2 = jnp.concatenate([rms_norm(xp, g_norm_ffn[l]).reshape(B * S, D),
                              rms_norm(xs, g_norm_ffn[l]).reshape(DB * DS, D)], axis=0)
        y = moe_ffn(h2, w_router_group[l], w_router_expert[l], w_exp_gate[l], w_exp_up[l], w_exp_down[l])
        xp = xp + y[:B * S].reshape(B, S, D)
        xs = xs + y[B * S:].reshape(DB, DS, D)

    return (xp, xs,
            jnp.stack(new_kv_p[0]), jnp.stack(new_kv_s[0]),
            jnp.stack(new_kv_p[1]), jnp.stack(new_kv_s[1]),
            jnp.stack(new_kv_p[2]), jnp.stack(new_kv_s[2]),
            jnp.stack(gla_p), jnp.stack(gla_s))
```

```python
import functools

import jax
import jax.numpy as jnp
from jax import lax
from jax.experimental import pallas as pl
from jax.experimental.pallas import tpu as pltpu

F32 = jnp.float32
BF16 = jnp.bfloat16
HIGHEST = lax.Precision.HIGHEST

LANES = 128
SUBLANES = 8
VMEM_LIMIT_BYTES = 56 * 1024 * 1024

EPS = 1e-6
NEG = -1e30

GLA_HEADS = 8
GLA_DK = 128
GLA_DV = 256
GLA_TAU = 16.0
GLA_RANK = 16
HEAD_DIM = 128
HEADS_PER_GROUP = 8
WINDOWS = (128, 512, 2048)
DILATIONS = (1, 4, 16)
N_GROUPS = 4
EXPERTS_PER_GROUP = 8
N_EXPERTS = N_GROUPS * EXPERTS_PER_GROUP
TOP_K = 2
Q_BLOCK = 128


def _cparams(sem):
    return pltpu.CompilerParams(dimension_semantics=sem, vmem_limit_bytes=VMEM_LIMIT_BYTES)


def _mm(a, b, prec, trans_b=False):
    dn = (((1,), (1 if trans_b else 0,)), ((), ()))
    if prec == 'bf16':
        return lax.dot_general(a.astype(BF16), b.astype(BF16), dn, preferred_element_type=F32)
    return lax.dot_general(a.astype(F32), b.astype(F32), dn, precision=HIGHEST,
                           preferred_element_type=F32)


def _round_bf16(a):
    return a.astype(BF16).astype(F32)


def _sigmoid(x):
    return 1.0 / (1.0 + jnp.exp(-x))


def _wspec(w, layer, blk, idx):
    if w.ndim == len(blk):
        return pl.BlockSpec(blk, idx)
    return pl.BlockSpec((None,) + tuple(blk), lambda *a: (layer,) + tuple(idx(*a)))


def _rmsnorm_kernel(x_ref, g_ref, o_ref):
    x = x_ref[...]
    y = x * lax.rsqrt(jnp.mean(x * x, axis=-1, keepdims=True) + EPS)
    o_ref[...] = (y * g_ref[...]).astype(o_ref.dtype)


def rmsnorm(x, g, layer, *, out_dtype, tm):
    m, d = x.shape
    g3 = g.reshape(g.shape[0], 1, d)
    return pl.pallas_call(
        _rmsnorm_kernel,
        grid=(m // tm,),
        in_specs=[pl.BlockSpec((tm, d), lambda i: (i, 0)),
                  pl.BlockSpec((None, 1, d), lambda i: (layer, 0, 0))],
        out_specs=pl.BlockSpec((tm, d), lambda i: (i, 0)),
        out_shape=jax.ShapeDtypeStruct((m, d), out_dtype),
        compiler_params=_cparams(("parallel",)),
        name="rmsnorm",
    )(x, g3)


def _matmul_kernel(*refs, prec, has_res):
    if has_res:
        x_ref, w_ref, r_ref, o_ref = refs
    else:
        x_ref, w_ref, o_ref = refs
    acc = _mm(x_ref[...], w_ref[...], prec)
    if has_res:
        acc = r_ref[...] + acc
    o_ref[...] = acc.astype(o_ref.dtype)


def matmul(x, w, *, prec, out_dtype, tm, tn, layer=None, col0=0, ncols=None, residual=None):
    m, k = x.shape
    ncols = w.shape[-1] if ncols is None else ncols
    assert m % tm == 0 and ncols % tn == 0 and col0 % tn == 0
    c0 = col0 // tn
    in_specs = [pl.BlockSpec((tm, k), lambda i, j: (i, 0)),
                _wspec(w, layer, (k, tn), lambda i, j: (0, j + c0))]
    args = [x, w]
    if residual is not None:
        in_specs.append(pl.BlockSpec((tm, tn), lambda i, j: (i, j)))
        args.append(residual)
    return pl.pallas_call(
        functools.partial(_matmul_kernel, prec=prec, has_res=residual is not None),
        grid=(m // tm, ncols // tn),
        in_specs=in_specs,
        out_specs=pl.BlockSpec((tm, tn), lambda i, j: (i, j)),
        out_shape=jax.ShapeDtypeStruct((m, ncols), out_dtype),
        compiler_params=_cparams(("parallel", "arbitrary")),
        name="matmul",
    )(*args)


def _bcast_rows(a, idxs):
    return jnp.concatenate(
        [jnp.broadcast_to(a[i:i + 1, :], (SUBLANES, a.shape[1])) for i in idxs], axis=0)


def _gla_kernel(q_ref, k_ref, v_ref, r_ref, glr_ref, wup_ref, bg_ref, gout_ref, s0_ref,
                o_ref, sout_ref, st_ref, *, chunk, prec):
    tb = pl.program_id(2)
    rows_per_step = q_ref.shape[0]
    n_chunks = rows_per_step // chunk
    nb = chunk // SUBLANES

    @pl.when(tb == 0)
    def _():
        st_ref[...] = s0_ref[...].T

    row = lax.broadcasted_iota(jnp.int32, (chunk, GLA_DK), 0)
    rr = lax.broadcasted_iota(jnp.int32, (chunk, chunk), 0)
    cc = lax.broadcasted_iota(jnp.int32, (chunk, chunk), 1)
    tril = (cc <= rr).astype(F32)
    levels = []
    m = chunk // 2
    while m >= SUBLANES:
        levels.append(m)
        m //= 2
    wup = wup_ref[...]
    bg = bg_ref[...]
    gout = gout_ref[...]
    scale = GLA_DK ** -0.5

    def chunk_body(c, carry):
        r0 = pl.multiple_of(c * chunk, chunk)
        rows = pl.ds(r0, chunk)
        q = q_ref[rows, :].astype(F32) * scale
        k = k_ref[rows, :].astype(F32)
        v = v_ref[rows, :].astype(F32)
        z = _mm(glr_ref[rows, :], wup, prec) + bg
        la = (jnp.minimum(z, 0.0) - jnp.log1p(jnp.exp(-jnp.abs(z)))) * (1.0 / GLA_TAU)
        b = _mm(tril, la, 'f32')
        st = st_ref[...]
        o = _mm(q * jnp.exp(b), st, prec, trans_b=True)

        a = jnp.zeros((chunk, chunk), F32)
        for m in levels:
            mb = m // SUBLANES
            sh = m.bit_length() - 1
            q_idx = [max((r // mb) * m - 1, 0) for r in range(nb)]
            k_idx = [(r // mb + 1) * m - 1 for r in range(nb)]
            q_role = ((row >> sh) & 1) == 1
            qt = q * jnp.exp(jnp.where(q_role, b - _bcast_rows(b, q_idx), NEG))
            kt = k * jnp.exp(jnp.where(q_role, NEG, _bcast_rows(b, k_idx) - b))
            a_l = _mm(qt, kt, 'f32', trans_b=True)
            a = a + jnp.where((cc >> sh) == (rr >> sh) - 1, a_l, 0.0)
        tl = row & (SUBLANES - 1)
        for j in range(SUBLANES):
            idx = [SUBLANES * r + j for r in range(nb)]
            e = jnp.exp(jnp.where(tl >= j, b - _bcast_rows(b, idx), NEG))
            w = jnp.sum(q * e * _bcast_rows(k, idx), axis=-1, keepdims=True)
            a = a + jnp.where(cc == (rr & -SUBLANES) + j, w, 0.0)
        o = o + _mm(a, v, prec)

        b_end = b[chunk - 1:chunk, :]
        khat = k * jnp.exp(b_end - b)
        dn = (((0,), (0,)), ((), ()))
        if prec == 'bf16':
            upd = lax.dot_general(v.astype(BF16), khat.astype(BF16), dn, preferred_element_type=F32)
        else:
            upd = lax.dot_general(v, khat, dn, precision=HIGHEST, preferred_element_type=F32)
        st_ref[...] = st * jnp.exp(b_end) + upd

        on = o * lax.rsqrt(jnp.mean(o * o, axis=-1, keepdims=True) + EPS) * gout
        r = r_ref[rows, :].astype(F32)
        o_ref[rows, :] = (on * (r * _sigmoid(r))).astype(o_ref.dtype)
        return carry

    lax.fori_loop(0, n_chunks, chunk_body, 0)

    @pl.when(tb == pl.num_programs(2) - 1)
    def _():
        sout_ref[...] = st_ref[...].T


def gla(pg, pglr, wup_pad, b_gate, g_out, s0, layer, *, batch, seq, chunk, rows_per_step,
        row0, prec, out_dtype):
    n = batch * seq
    steps = seq // rows_per_step
    kq = GLA_HEADS * GLA_DK // GLA_DK
    kv = 2 * GLA_HEADS * GLA_DK // GLA_DV
    kr = kv + GLA_HEADS
    blk0 = row0 // rows_per_step
    row_map = lambda b, h, t: blk0 + b * steps + t
    out_map = lambda b, h, t: b * steps + t
    bg3 = b_gate.reshape(b_gate.shape[0], 1, -1)
    go3 = g_out.reshape(g_out.shape[0], 1, -1)
    if s0.ndim == 4:
        s0_spec = pl.BlockSpec((None, None, GLA_DK, GLA_DV), lambda b, h, t: (b, h, 0, 0))
    else:
        s0_spec = pl.BlockSpec((None, None, None, GLA_DK, GLA_DV), lambda b, h, t: (layer, b, h, 0, 0))
    return pl.pallas_call(
        functools.partial(_gla_kernel, chunk=chunk, prec=prec),
        grid=(batch, GLA_HEADS, steps),
        in_specs=[
            pl.BlockSpec((rows_per_step, GLA_DK), lambda b, h, t: (row_map(b, h, t), h)),
            pl.BlockSpec((rows_per_step, GLA_DK), lambda b, h, t: (row_map(b, h, t), kq + h)),
            pl.BlockSpec((rows_per_step, GLA_DV), lambda b, h, t: (row_map(b, h, t), kv + h)),
            pl.BlockSpec((rows_per_step, GLA_DV), lambda b, h, t: (row_map(b, h, t), kr + h)),
            pl.BlockSpec((rows_per_step, LANES), lambda b, h, t: (row_map(b, h, t), 0)),
            pl.BlockSpec((None, LANES, GLA_DK), lambda b, h, t: (layer, 0, h)),
            pl.BlockSpec((None, 1, GLA_DK), lambda b, h, t: (layer, 0, h)),
            pl.BlockSpec((None, 1, GLA_DV), lambda b, h, t: (layer, 0, 0)),
            s0_spec,
        ],
        out_specs=[
            pl.BlockSpec((rows_per_step, GLA_DV), lambda b, h, t: (out_map(b, h, t), h)),
            pl.BlockSpec((None, None, GLA_DK, GLA_DV), lambda b, h, t: (b, h, 0, 0)),
        ],
        out_shape=[jax.ShapeDtypeStruct((n, GLA_HEADS * GLA_DV), out_dtype),
                   jax.ShapeDtypeStruct((batch, GLA_HEADS, GLA_DK, GLA_DV), F32)],
        scratch_shapes=[pltpu.VMEM((GLA_DV, GLA_DK), F32)],
        compiler_params=_cparams(("parallel", "parallel", "arbitrary")),
        name="gla",
    )(pg, pg, pg, pg, pglr, wup_pad, bg3, go3, s0)


def _head_norm(x, g):
    return x * lax.rsqrt(jnp.mean(x * x, axis=-1, keepdims=True) + EPS) * g


def _attn_prompt_kernel(*refs, seq, prec):
    (q0, q1, q2, k0, k1, k2, v0, v1, v2, gq_ref, gk_ref,
     att_ref, kc0, vc0, kc1, vc1, kc2, vc2, qn_s, kn_s, o_s, l_s) = refs
    q_refs, k_refs, v_refs = (q0, q1, q2), (k0, k1, k2), (v0, v1, v2)
    kc_refs, vc_refs = (kc0, kc1, kc2), (vc0, vc1, vc2)
    gq = gq_ref[...]
    gk = gk_ref[...]
    qb_rows = Q_BLOCK
    ri = lax.broadcasted_iota(jnp.int32, (qb_rows, qb_rows), 0)
    ci = lax.broadcasted_iota(jnp.int32, (qb_rows, qb_rows), 1)
    cur_ok = ci <= ri
    prev_ok = ci >= ri

    for g in range(3):
        w = min(WINDOWS[g], seq)
        qn_s[g] = _head_norm(q_refs[g][...].astype(F32), gq) * (HEAD_DIM ** -0.5)
        kn = _head_norm(k_refs[g][...].astype(F32), gk)
        kn_s[g] = kn
        kc_refs[g][...] = kn[seq - w:, :]
        vc_refs[g][...] = v_refs[g][seq - w:, :].astype(F32)

    for g in range(3):
        d = DILATIONS[g]
        assert WINDOWS[g] // d == qb_rows
        sub_len = seq // d
        nqb = sub_len // qb_rows
        qn_g, kn_g, v_g, o_g, l_g = qn_s.at[g], kn_s.at[g], v_refs[g], o_s.at[g], l_s.at[g]

        def block(i, carry, d=d, nqb=nqb, qn_g=qn_g, kn_g=kn_g, v_g=v_g, o_g=o_g, l_g=l_g):
            res = i // nqb
            qb = i % nqb
            cur = pl.ds(res + d * qb_rows * qb, qb_rows, stride=d)
            q = qn_g[cur, :]
            kc = kn_g[cur, :]
            vc = v_g[cur, :].astype(F32)
            s_cur = jnp.where(cur_ok, _mm(q, kc, prec, trans_b=True), NEG)
            m = jnp.max(s_cur, axis=-1, keepdims=True)
            if nqb > 1:
                pb = jnp.maximum(qb - 1, 0)
                prev = pl.ds(res + d * qb_rows * pb, qb_rows, stride=d)
                kp = kn_g[prev, :]
                vp = v_g[prev, :].astype(F32)
                s_prev = jnp.where(prev_ok, _mm(q, kp, prec, trans_b=True), NEG)
                s_prev = s_prev + jnp.where(qb > 0, 0.0, NEG)
                m = jnp.maximum(m, jnp.max(s_prev, axis=-1, keepdims=True))
            p_cur = jnp.exp(s_cur - m)
            den = jnp.sum(p_cur, axis=-1, keepdims=True)
            if nqb > 1:
                p_prev = jnp.exp(s_prev - m)
                den = den + jnp.sum(p_prev, axis=-1, keepdims=True)
            acc = _mm(p_cur / den, vc, prec)
            if nqb > 1:
                acc = acc + _mm(p_prev / den, vp, prec)
            o_g[cur, :] = acc
            l_g[cur, :] = jnp.broadcast_to(m + jnp.log(den), (qb_rows, HEAD_DIM))
            return carry

        lax.fori_loop(0, d * nqb, block, 0)

    def combine(i, carry):
        rows = pl.ds(pl.multiple_of(i * qb_rows, qb_rows), qb_rows)
        l0, l1, l2 = l_s[0, rows, :], l_s[1, rows, :], l_s[2, rows, :]
        m = jnp.maximum(jnp.maximum(l0, l1), l2)
        e0, e1, e2 = jnp.exp(l0 - m), jnp.exp(l1 - m), jnp.exp(l2 - m)
        tot = e0 + e1 + e2
        out = o_s[0, rows, :] * (e0 / tot) + o_s[1, rows, :] * (e1 / tot) + o_s[2, rows, :] * (e2 / tot)
        att_ref[rows, :] = out.astype(att_ref.dtype)
        return carry

    lax.fori_loop(0, seq // qb_rows, combine, 0)


def attn_prompt(pa, g_q, g_k, layer, *, batch, seq, prec, out_dtype):
    hpg = HEADS_PER_GROUP
    nh = 3 * hpg
    gq3 = g_q.reshape(g_q.shape[0], 1, -1)
    gk3 = g_k.reshape(g_k.shape[0], 1, -1)
    in_specs = []
    for part in range(3):
        for g in range(3):
            in_specs.append(pl.BlockSpec(
                (seq, HEAD_DIM), lambda b, j, part=part, g=g: (b, part * nh + g * hpg + j)))
    in_specs += [pl.BlockSpec((None, 1, HEAD_DIM), lambda b, j: (layer, 0, 0))] * 2
    out_specs = [pl.BlockSpec((seq, HEAD_DIM), lambda b, j: (b, j))]
    out_shape = [jax.ShapeDtypeStruct((batch * seq, hpg * HEAD_DIM), out_dtype)]
    for g in range(3):
        w = min(WINDOWS[g], seq)
        for _ in range(2):
            out_specs.append(pl.BlockSpec((None, w, HEAD_DIM), lambda b, j: (b, 0, j)))
            out_shape.append(jax.ShapeDtypeStruct((batch, w, hpg * HEAD_DIM), F32))
    res = pl.pallas_call(
        functools.partial(_attn_prompt_kernel, seq=seq, prec=prec),
        grid=(batch, hpg),
        in_specs=in_specs,
        out_specs=out_specs,
        out_shape=out_shape,
        scratch_shapes=[pltpu.VMEM((3, seq, HEAD_DIM), F32)] * 4,
        compiler_params=_cparams(("parallel", "parallel")),
        name="attn_prompt",
    )(*([pa] * 9), gq3, gk3)
    return res[0], [(res[1 + 2 * g], res[2 + 2 * g]) for g in range(3)]


def _attn_sample_kernel(*refs, n_new, prec):
    (q0, q1, q2, k0, k1, k2, v0, v1, v2, ck0, cv0, ck1, cv1, ck2, cv2, gq_ref, gk_ref,
     att_ref, kn0, vn0, kn1, vn1, kn2, vn2) = refs
    q_refs, k_refs, v_refs = (q0, q1, q2), (k0, k1, k2), (v0, v1, v2)
    ck_refs, cv_refs = (ck0, ck1, ck2), (cv0, cv1, cv2)
    kn_refs, vn_refs = (kn0, kn1, kn2), (vn0, vn1, vn2)
    gq = gq_ref[...]
    gk = gk_ref[...]
    rnd = _round_bf16 if prec == 'bf16' else (lambda a: a)
    outs, lses = [], []
    for g in range(3):
        d = DILATIONS[g]
        nk = WINDOWS[g] // d
        clen = ck_refs[g].shape[0]
        q = _head_norm(q_refs[g][...], gq) * (HEAD_DIM ** -0.5)
        kn = _head_norm(k_refs[g][...], gk)
        vn = v_refs[g][...]
        kn_refs[g][...] = kn
        vn_refs[g][...] = vn
        s_c = _mm(q, ck_refs[g][...], prec, trans_b=True)
        t_c = lax.broadcasted_iota(jnp.int32, (n_new, clen), 0)
        n_c = lax.broadcasted_iota(jnp.int32, (n_new, clen), 1)
        off = clen + t_c - n_c
        s_c = jnp.where(((off & (d - 1)) == 0) & (off <= nk * d), s_c, NEG)
        t_n = lax.broadcasted_iota(jnp.int32, (n_new, n_new), 0)
        n_n = lax.broadcasted_iota(jnp.int32, (n_new, n_new), 1)
        s_n = jnp.zeros((n_new, n_new), F32)
        q_r, kn_r, vn_r = rnd(q), rnd(kn), rnd(vn)
        for j in range(n_new):
            col = jnp.sum(q_r * kn_r[j:j + 1, :], axis=-1, keepdims=True)
            s_n = s_n + jnp.where(n_n == j, col, 0.0)
        off_n = t_n - n_n
        s_n = jnp.where((off_n >= 0) & ((off_n & (d - 1)) == 0) & (off_n <= nk * d), s_n, NEG)
        m = jnp.maximum(jnp.max(s_c, axis=-1, keepdims=True), jnp.max(s_n, axis=-1, keepdims=True))
        p_c = jnp.exp(s_c - m)
        p_n = jnp.exp(s_n - m)
        den = jnp.sum(p_c, axis=-1, keepdims=True) + jnp.sum(p_n, axis=-1, keepdims=True)
        acc = _mm(p_c / den, cv_refs[g][...], prec)
        pn_r = rnd(p_n / den)
        for j in range(n_new):
            acc = acc + pn_r[:, j:j + 1] * vn_r[j:j + 1, :]
        outs.append(acc)
        lses.append(m + jnp.log(den))
    m = jnp.maximum(jnp.maximum(lses[0], lses[1]), lses[2])
    es = [jnp.exp(l - m) for l in lses]
    tot = es[0] + es[1] + es[2]
    att_ref[...] = (outs[0] * (es[0] / tot) + outs[1] * (es[1] / tot)
                    + outs[2] * (es[2] / tot)).astype(att_ref.dtype)


def attn_sample(pa, caches, g_q, g_k, layer, *, batch, n_new, row0, prec, out_dtype):
    hpg = HEADS_PER_GROUP
    nh = 3 * hpg
    gq3 = g_q.reshape(g_q.shape[0], 1, -1)
    gk3 = g_k.reshape(g_k.shape[0], 1, -1)
    blk0 = row0 // n_new
    in_specs = []
    for part in range(3):
        for g in range(3):
            in_specs.append(pl.BlockSpec(
                (n_new, HEAD_DIM), lambda b, j, part=part, g=g: (blk0 + b, part * nh + g * hpg + j)))
    cache_args = []
    for g in range(3):
        c = caches[g]
        clen = c.shape[2]
        c4 = c.reshape(c.shape[0], c.shape[1], clen, 2 * hpg * HEAD_DIM)
        for kv in range(2):
            in_specs.append(pl.BlockSpec((None, None, clen, HEAD_DIM),
                                         lambda b, j, kv=kv: (layer, b, 0, kv * hpg + j)))
            cache_args.append(c4)
    in_specs += [pl.BlockSpec((None, 1, HEAD_DIM), lambda b, j: (layer, 0, 0))] * 2
    out_specs = [pl.BlockSpec((n_new, HEAD_DIM), lambda b, j: (b, j))]
    out_shape = [jax.ShapeDtypeStruct((batch * n_new, hpg * HEAD_DIM), out_dtype)]
    for g in range(3):
        for _ in range(2):
            out_specs.append(pl.BlockSpec((n_new, HEAD_DIM), lambda b, j: (b, j)))
            out_shape.append(jax.ShapeDtypeStruct((batch * n_new, hpg * HEAD_DIM), F32))
    res = pl.pallas_call(
        functools.partial(_attn_sample_kernel, n_new=n_new, prec=prec),
        grid=(batch, hpg),
        in_specs=in_specs,
        out_specs=out_specs,
        out_shape=out_shape,
        compiler_params=_cparams(("parallel", "parallel")),
        name="attn_sample",
    )(*([pa] * 9), *cache_args, gq3, gk3)
    return res[0], [(res[1 + 2 * g], res[2 + 2 * g]) for g in range(3)]


def _merge_kernel(h_ref, og_ref, at_ref, wg1_ref, wg2_ref, wgp_ref, wap_ref, o_ref, *, prec):
    h = h_ref[...]
    g1 = _mm(h, wg1_ref[...], prec)
    g2 = _mm(h, wg2_ref[...], prec)
    a = _mm(og_ref[...], wgp_ref[...], prec)
    b = _mm(at_ref[...], wap_ref[...], prec)
    o_ref[...] = (_sigmoid(g1) * a + _sigmoid(g2) * b).astype(o_ref.dtype)


def merge(h, og, att, wmg, wgp, wap, layer, *, prec, out_dtype, tm, tn):
    m, d = h.shape
    nj = d // tn
    return pl.pallas_call(
        functools.partial(_merge_kernel, prec=prec),
        grid=(m // tm, nj),
        in_specs=[
            pl.BlockSpec((tm, d), lambda i, j: (i, 0)),
            pl.BlockSpec((tm, og.shape[1]), lambda i, j: (i, 0)),
            pl.BlockSpec((tm, att.shape[1]), lambda i, j: (i, 0)),
            _wspec(wmg, layer, (d, tn), lambda i, j: (0, j)),
            _wspec(wmg, layer, (d, tn), lambda i, j: (0, nj + j)),
            _wspec(wgp, layer, (og.shape[1], tn), lambda i, j: (0, j)),
            _wspec(wap, layer, (att.shape[1], tn), lambda i, j: (0, j)),
        ],
        out_specs=pl.BlockSpec((tm, tn), lambda i, j: (i, j)),
        out_shape=jax.ShapeDtypeStruct((m, d), out_dtype),
        compiler_params=_cparams(("parallel", "arbitrary")),
        name="merge",
    )(h, og, att, wmg, wmg, wgp, wap)


def _router_kernel(x_ref, g_ref, wr_ref, h_ref, route_ref):
    x = x_ref[...]
    h = x * lax.rsqrt(jnp.mean(x * x, axis=-1, keepdims=True) + EPS) * g_ref[...]
    h_ref[...] = h.astype(h_ref.dtype)
    logits = _mm(h, wr_ref[...], 'bf16')
    lane = lax.broadcasted_iota(jnp.int32, logits.shape, 1)
    lane_f = lane.astype(F32)
    big = float(LANES)
    is_g = lane < N_GROUPS
    gl = jnp.where(is_g, logits, NEG)
    gmax = jnp.max(gl, axis=-1, keepdims=True)
    gsel = jnp.min(jnp.where(gl == gmax, lane_f, big), axis=-1, keepdims=True)
    gprob = 1.0 / jnp.sum(jnp.where(is_g, jnp.exp(gl - gmax), 0.0), axis=-1, keepdims=True)
    e_idx = lane - N_GROUPS
    e_grp = e_idx >> (EXPERTS_PER_GROUP.bit_length() - 1)
    in_grp = (e_idx >= 0) & (e_idx < N_EXPERTS) & (e_grp.astype(F32) == gsel)
    el = jnp.where(in_grp, logits, NEG)
    v1 = jnp.max(el, axis=-1, keepdims=True)
    i1 = jnp.min(jnp.where(el == v1, lane_f, big), axis=-1, keepdims=True)
    el2 = jnp.where(lane_f == i1, NEG, el)
    v2 = jnp.max(el2, axis=-1, keepdims=True)
    i2 = jnp.min(jnp.where(el2 == v2, lane_f, big), axis=-1, keepdims=True)
    t = jnp.exp(v2 - v1)
    w1 = gprob / (1.0 + t)
    w2 = gprob * (t / (1.0 + t))
    route = jnp.where(lane == 0, i1 - N_GROUPS,
                      jnp.where(lane == 1, i2 - N_GROUPS,
                                jnp.where(lane == 2, w1, jnp.where(lane == 3, w2, 0.0))))
    route_ref[...] = route


def router(x, g, wr, layer, *, tm):
    m, d = x.shape
    g3 = g.reshape(g.shape[0], 1, d)
    return pl.pallas_call(
        _router_kernel,
        grid=(m // tm,),
        in_specs=[pl.BlockSpec((tm, d), lambda i: (i, 0)),
                  pl.BlockSpec((None, 1, d), lambda i: (layer, 0, 0)),
                  pl.BlockSpec((None, d, LANES), lambda i: (layer, 0, 0))],
        out_specs=[pl.BlockSpec((tm, d), lambda i: (i, 0)),
                   pl.BlockSpec((tm, LANES), lambda i: (i, 0))],
        out_shape=[jax.ShapeDtypeStruct((m, d), F32),
                   jax.ShapeDtypeStruct((m, LANES), F32)],
        compiler_params=_cparams(("parallel",)),
        name="router",
    )(x, g3, wr)


def _row_copy(src_hbm, dst, sem, src_row, dst_row):
    return pltpu.make_async_copy(src_hbm.at[pl.ds(src_row, 1)], dst.at[pl.ds(dst_row, 1)], sem)


def _gather_rows_kernel(idx_ref, src_hbm, o_ref, sem):
    rb = o_ref.shape[0]
    base = pl.program_id(0) * rb

    def start(i, c):
        _row_copy(src_hbm, o_ref, sem, idx_ref[base + i], i).start()
        return c

    def wait(i, c):
        _row_copy(src_hbm, o_ref, sem, idx_ref[base + i], i).wait()
        return c

    lax.fori_loop(0, rb, start, 0)
    lax.fori_loop(0, rb, wait, 0)


def gather_rows(src, idx, *, rb):
    rows = idx.shape[0]
    d = src.shape[1]
    return pl.pallas_call(
        _gather_rows_kernel,
        grid_spec=pltpu.PrefetchScalarGridSpec(
            num_scalar_prefetch=1,
            grid=(rows // rb,),
            in_specs=[pl.BlockSpec(memory_space=pl.ANY)],
            out_specs=pl.BlockSpec((rb, d), lambda i, idx: (i, 0)),
            scratch_shapes=[pltpu.SemaphoreType.DMA(())],
        ),
        out_shape=jax.ShapeDtypeStruct((rows, d), src.dtype),
        compiler_params=_cparams(("arbitrary",)),
        name="moe_gather",
    )(idx, src)


def _moe_up_kernel(blk_e_ref, nused_ref, x_ref, wg_ref, wu_ref, h_ref, *, prec):
    used = pl.program_id(1) < nused_ref[0]

    @pl.when(used)
    def _():
        x = x_ref[...]
        g = _mm(x, wg_ref[...], prec)
        u = _mm(x, wu_ref[...], prec)
        h_ref[...] = (g * _sigmoid(g) * u).astype(h_ref.dtype)

    @pl.when(jnp.logical_not(used))
    def _():
        h_ref[...] = jnp.zeros_like(h_ref)


def _moe_down_kernel(blk_e_ref, nused_ref, h_ref, wd_ref, rw_ref, y_ref, *, prec):
    used = pl.program_id(1) < nused_ref[0]

    @pl.when(used)
    def _():
        y_ref[...] = _mm(h_ref[...], wd_ref[...], prec) * rw_ref[...]

    @pl.when(jnp.logical_not(used))
    def _():
        y_ref[...] = jnp.zeros_like(y_ref)


def moe_experts(xs, row_w, blk_e, n_used, w_gate, w_up, w_down, layer, *, rb, tf, tn, prec, h_dtype):
    rows, d = xs.shape
    f = w_gate.shape[-1]
    nb = rows // rb
    hmid = pl.pallas_call(
        functools.partial(_moe_up_kernel, prec=prec),
        grid_spec=pltpu.PrefetchScalarGridSpec(
            num_scalar_prefetch=2,
            grid=(f // tf, nb),
            in_specs=[
                pl.BlockSpec((rb, d), lambda j, i, be, nu: (i, 0)),
                pl.BlockSpec((None, None, d, tf), lambda j, i, be, nu: (layer, be[i], 0, j)),
                pl.BlockSpec((None, None, d, tf), lambda j, i, be, nu: (layer, be[i], 0, j)),
            ],
            out_specs=pl.BlockSpec((rb, tf), lambda j, i, be, nu: (i, j)),
        ),
        out_shape=jax.ShapeDtypeStruct((rows, f), h_dtype),
        compiler_params=_cparams(("arbitrary", "arbitrary")),
        name="moe_up",
    )(blk_e, n_used, xs, w_gate, w_up)
    return pl.pallas_call(
        functools.partial(_moe_down_kernel, prec=prec),
        grid_spec=pltpu.PrefetchScalarGridSpec(
            num_scalar_prefetch=2,
            grid=(d // tn, nb),
            in_specs=[
                pl.BlockSpec((rb, f), lambda j, i, be, nu: (i, 0)),
                pl.BlockSpec((None, None, f, tn), lambda j, i, be, nu: (layer, be[i], 0, j)),
                pl.BlockSpec((rb, 1), lambda j, i, be, nu: (i, 0)),
            ],
            out_specs=pl.BlockSpec((rb, tn), lambda j, i, be, nu: (i, j)),
        ),
        out_shape=jax.ShapeDtypeStruct((rows, d), F32),
        compiler_params=_cparams(("arbitrary", "arbitrary")),
        name="moe_down",
    )(blk_e, n_used, hmid, w_down, row_w)


def _combine_kernel(pos_ref, x_ref, y_hbm, o_ref, buf, sem):
    tm = x_ref.shape[0]
    base = pl.program_id(0) * tm

    def start(i, c):
        for k in range(TOP_K):
            _row_copy(y_hbm, buf.at[k], sem, pos_ref[TOP_K * (base + i) + k], i).start()
        return c

    def wait(i, c):
        for k in range(TOP_K):
            _row_copy(y_hbm, buf.at[k], sem, pos_ref[TOP_K * (base + i) + k], i).wait()
        return c

    lax.fori_loop(0, tm, start, 0)
    lax.fori_loop(0, tm, wait, 0)
    o_ref[...] = x_ref[...] + (buf[0] + buf[1])


def moe_combine(x, y, pos, *, tm):
    m, d = x.shape
    return pl.pallas_call(
        _combine_kernel,
        grid_spec=pltpu.PrefetchScalarGridSpec(
            num_scalar_prefetch=1,
            grid=(m // tm,),
            in_specs=[pl.BlockSpec((tm, d), lambda i, pos: (i, 0)),
                      pl.BlockSpec(memory_space=pl.ANY)],
            out_specs=pl.BlockSpec((tm, d), lambda i, pos: (i, 0)),
            scratch_shapes=[pltpu.VMEM((TOP_K, tm, d), F32), pltpu.SemaphoreType.DMA(())],
        ),
        out_shape=jax.ShapeDtypeStruct((m, d), F32),
        compiler_params=_cparams(("arbitrary",)),
        name="moe_combine",
    )(pos, x, y)


def _dispatch_plan(route, rb):
    n = route.shape[0]
    eid = route[:, :TOP_K].astype(jnp.int32).reshape(-1)
    gate = route[:, TOP_K:2 * TOP_K].reshape(-1)
    npair = n * TOP_K
    order = jnp.argsort(eid, stable=True)
    counts = jnp.sum(eid[:, None] == jnp.arange(N_EXPERTS, dtype=jnp.int32)[None, :], axis=0,
                     dtype=jnp.int32)
    starts = jnp.cumsum(counts) - counts
    padded = (counts + rb - 1) // rb * rb
    pend = jnp.cumsum(padded)
    pstarts = pend - padded
    nb = (npair + N_EXPERTS * (rb - 1) + rb - 1) // rb
    blk_e = jnp.minimum(jnp.searchsorted(pend, jnp.arange(nb, dtype=jnp.int32) * rb, side='right'),
                        N_EXPERTS - 1).astype(jnp.int32)
    p = jnp.arange(nb * rb, dtype=jnp.int32)
    pe = blk_e[p // rb]
    off = p - pstarts[pe]
    real = off < counts[pe]
    src = order[jnp.clip(starts[pe] + off, 0, npair - 1)]
    row_tok = jnp.where(real, src // TOP_K, 0).astype(jnp.int32)
    row_w = jnp.where(real, gate[src], 0.0).astype(F32)[:, None]
    eid_s = eid[order]
    dest = pstarts[eid_s] + (jnp.arange(npair, dtype=jnp.int32) - starts[eid_s])
    pos = jnp.zeros((npair,), jnp.int32).at[order].set(dest.astype(jnp.int32))
    n_used = (pend[-1:] // rb).astype(jnp.int32)
    return row_tok, row_w, blk_e, n_used, pos


def moe_layer(x, g, wr, w_gate, w_up, w_down, layer, *, prec, rb, tf, tn, tm_router, tm_combine):
    h2, route = router(x, g, wr, layer, tm=tm_router)
    row_tok, row_w, blk_e, n_used, pos = _dispatch_plan(route, rb)
    xs = gather_rows(h2, row_tok, rb=rb)
    h_dtype = BF16 if prec == 'bf16' else F32
    y = moe_experts(xs, row_w, blk_e, n_used, w_gate, w_up, w_down, layer,
                    rb=rb, tf=tf, tn=tn, prec=prec, h_dtype=h_dtype)
    return moe_combine(x, y, pos, tm=tm_combine)


GLA_COLS = 2 * GLA_HEADS * GLA_DK + 2 * GLA_HEADS * GLA_DV
ATT_COLS = 3 * 3 * HEADS_PER_GROUP * HEAD_DIM


PREC = 'bf16'
MOE_ROW_BLOCK = 128
BF16_ROWS = 16


def _row_tile(n, target):
    best = None
    for t in range(BF16_ROWS, target + 1, BF16_ROWS):
        if n % t == 0:
            best = t
    assert best is not None, (n, target)
    return best


def kernel(x_prompt, x_sample, cache_kv_w128, cache_kv_w512, cache_kv_w2048, state_gla,
           g_norm_mix, w_in, w_gla_gate_up, b_gla_gate, g_q, g_k, g_gla_out,
           w_gla_proj, w_att_proj, w_merge_gate, w_out, g_norm_ffn,
           w_router_group, w_router_expert, w_exp_gate, w_exp_up, w_exp_down):
    caches = (cache_kv_w128, cache_kv_w512, cache_kv_w2048)
    bp, sp, d = x_prompt.shape
    bs, ss, _ = x_sample.shape
    depth = w_in.shape[0]
    hpg = HEADS_PER_GROUP
    n_p, n_s = bp * sp, bs * ss
    n = n_p + n_s
    x = jnp.concatenate([x_prompt.reshape(n_p, d), x_sample.reshape(n_s, d)], axis=0)
    tm_norm = _row_tile(n, 256)
    tm_mm = _row_tile(n, 1536)
    tm_merge = _row_tile(n, 768)
    tm_combine = _row_tile(n, 128)

    glr0 = GLA_COLS
    att0 = GLA_COLS + GLA_RANK
    wup_pad = jnp.pad(w_gla_gate_up, ((0, 0), (0, LANES - GLA_RANK), (0, 0)))
    wr = jnp.pad(jnp.concatenate([w_router_group, w_router_expert], axis=-1),
                 ((0, 0), (0, 0), (0, LANES - N_GROUPS - N_EXPERTS)))
    zero_state = jnp.zeros((bp, GLA_HEADS, GLA_DK, GLA_DV), F32)

    kv_p = [[] for _ in range(3)]
    kv_s = [[] for _ in range(3)]
    st_p, st_s = [], []
    for l in range(depth):
        w1 = w_in[l, :, :glr0].astype(BF16)
        wglr = jnp.pad(w_in[l, :, glr0:att0], ((0, 0), (0, LANES - GLA_RANK))).astype(BF16)
        w3 = w_in[l, :, att0:].astype(BF16)
        wmg = w_merge_gate[l].astype(BF16)
        wgp = w_gla_proj[l].astype(BF16)
        wap = w_att_proj[l].astype(BF16)
        wout = w_out[l].astype(BF16)

        h = rmsnorm(x, g_norm_mix, l, out_dtype=BF16, tm=tm_norm)
        pg = matmul(h, w1, prec=PREC, out_dtype=F32, tm=tm_mm, tn=512)
        pglr = matmul(h, wglr, prec=PREC, out_dtype=F32, tm=tm_mm, tn=LANES)
        pa = matmul(h, w3, prec=PREC, out_dtype=F32, tm=tm_mm, tn=512)

        og_p, st = gla(pg, pglr, wup_pad, b_gla_gate, g_gla_out, zero_state, l, batch=bp, seq=sp,
                       chunk=64, rows_per_step=512, row0=0, prec=PREC, out_dtype=BF16)
        st_p.append(st)
        og_s, st = gla(pg, pglr, wup_pad, b_gla_gate, g_gla_out, state_gla, l, batch=bs, seq=ss,
                       chunk=ss, rows_per_step=ss, row0=n_p, prec=PREC, out_dtype=F32)
        st_s.append(st)

        att_p, kvs = attn_prompt(pa, g_q, g_k, l, batch=bp, seq=sp, prec=PREC, out_dtype=BF16)
        for g in range(3):
            w = min(WINDOWS[g], sp)
            kv_p[g].append(jnp.stack([kvs[g][0].reshape(bp, w, hpg, HEAD_DIM),
                                      kvs[g][1].reshape(bp, w, hpg, HEAD_DIM)], axis=2))
        att_s, kvs = attn_sample(pa, caches, g_q, g_k, l, batch=bs, n_new=ss, row0=n_p,
                                 prec=PREC, out_dtype=F32)
        for g in range(3):
            kv_s[g].append(jnp.stack([kvs[g][0].reshape(bs, ss, hpg, HEAD_DIM),
                                      kvs[g][1].reshape(bs, ss, hpg, HEAD_DIM)], axis=2))

        og = jnp.concatenate([og_p, og_s.astype(BF16)], axis=0)
        att = jnp.concatenate([att_p, att_s.astype(BF16)], axis=0)
        mixed = merge(h, og, att, wmg, wgp, wap, None, prec=PREC, out_dtype=BF16, tm=tm_merge, tn=256)
        x = matmul(mixed, wout, prec=PREC, out_dtype=F32, tm=tm_mm, tn=512, residual=x)
        x = moe_layer(x, g_norm_ffn, wr, w_exp_gate, w_exp_up, w_exp_down, l, prec=PREC,
                      rb=MOE_ROW_BLOCK, tf=512, tn=2048, tm_router=tm_norm, tm_combine=tm_combine)

    xp, xs = x[:n_p], x[n_p:]
    return (xp.reshape(bp, sp, d), xs.reshape(bs, ss, d),
            jnp.stack(kv_p[0]), jnp.stack(kv_s[0]),
            jnp.stack(kv_p[1]), jnp.stack(kv_s[1]),
            jnp.stack(kv_p[2]), jnp.stack(kv_s[2]),
            jnp.stack(st_p), jnp.stack(st_s))
```

```python
import functools

import jax
import jax.numpy as jnp
from jax import lax
from jax.experimental import pallas as pl
from jax.experimental.pallas import tpu as pltpu

F32 = jnp.float32
BF16 = jnp.bfloat16
HIGHEST = lax.Precision.HIGHEST

LANES = 128
SUBLANES = 8
VMEM_LIMIT_BYTES = 56 * 1024 * 1024

EPS = 1e-6
NEG = -1e30

GLA_HEADS = 8
GLA_DK = 128
GLA_DV = 256
GLA_TAU = 16.0
GLA_RANK = 16
HEAD_DIM = 128
HEADS_PER_GROUP = 8
WINDOWS = (128, 512, 2048)
DILATIONS = (1, 4, 16)
N_GROUPS = 4
EXPERTS_PER_GROUP = 8
N_EXPERTS = N_GROUPS * EXPERTS_PER_GROUP
TOP_K = 2
Q_BLOCK = 128


def _cparams(sem):
    return pltpu.CompilerParams(dimension_semantics=sem, vmem_limit_bytes=VMEM_LIMIT_BYTES)


def _mm(a, b, prec, trans_b=False):
    dn = (((1,), (1 if trans_b else 0,)), ((), ()))
    if prec == 'bf16':
        return lax.dot_general(a.astype(BF16), b.astype(BF16), dn, preferred_element_type=F32)
    return lax.dot_general(a.astype(F32), b.astype(F32), dn, precision=HIGHEST,
                           preferred_element_type=F32)


def _round_bf16(a):
    return a.astype(BF16).astype(F32)


def _sigmoid(x):
    return 1.0 / (1.0 + jnp.exp(-x))


def _wspec(w, layer, blk, idx):
    if w.ndim == len(blk):
        return pl.BlockSpec(blk, idx)
    return pl.BlockSpec((None,) + tuple(blk), lambda *a: (layer,) + tuple(idx(*a)))


def _rmsnorm_kernel(x_ref, g_ref, o_ref):
    x = x_ref[...]
    y = x * lax.rsqrt(jnp.mean(x * x, axis=-1, keepdims=True) + EPS)
    o_ref[...] = (y * g_ref[...]).astype(o_ref.dtype)


def rmsnorm(x, g, layer, *, out_dtype, tm):
    m, d = x.shape
    g3 = g.reshape(g.shape[0], 1, d)
    return pl.pallas_call(
        _rmsnorm_kernel,
        grid=(m // tm,),
        in_specs=[pl.BlockSpec((tm, d), lambda i: (i, 0)),
                  pl.BlockSpec((None, 1, d), lambda i: (layer, 0, 0))],
        out_specs=pl.BlockSpec((tm, d), lambda i: (i, 0)),
        out_shape=jax.ShapeDtypeStruct((m, d), out_dtype),
        compiler_params=_cparams(("parallel",)),
        name="rmsnorm",
    )(x, g3)


def _matmul_kernel(*refs, prec, has_res):
    if has_res:
        x_ref, w_ref, r_ref, o_ref = refs
    else:
        x_ref, w_ref, o_ref = refs
    acc = _mm(x_ref[...], w_ref[...], prec)
    if has_res:
        acc = r_ref[...] + acc
    o_ref[...] = acc.astype(o_ref.dtype)


def matmul(x, w, *, prec, out_dtype, tm, tn, layer=None, col0=0, ncols=None, residual=None):
    m, k = x.shape
    ncols = w.shape[-1] if ncols is None else ncols
    assert m % tm == 0 and ncols % tn == 0 and col0 % tn == 0
    c0 = col0 // tn
    in_specs = [pl.BlockSpec((tm, k), lambda i, j: (i, 0)),
                _wspec(w, layer, (k, tn), lambda i, j: (0, j + c0))]
    args = [x, w]
    if residual is not None:
        in_specs.append(pl.BlockSpec((tm, tn), lambda i, j: (i, j)))
        args.append(residual)
    return pl.pallas_call(
        functools.partial(_matmul_kernel, prec=prec, has_res=residual is not None),
        grid=(m // tm, ncols // tn),
        in_specs=in_specs,
        out_specs=pl.BlockSpec((tm, tn), lambda i, j: (i, j)),
        out_shape=jax.ShapeDtypeStruct((m, ncols), out_dtype),
        compiler_params=_cparams(("parallel", "arbitrary")),
        name="matmul",
    )(*args)


def _bcast_rows(a, idxs):
    return jnp.concatenate(
        [jnp.broadcast_to(a[i:i + 1, :], (SUBLANES, a.shape[1])) for i in idxs], axis=0)


def _gla_kernel(q_ref, k_ref, v_ref, r_ref, glr_ref, wup_ref, bg_ref, gout_ref, s0_ref,
                o_ref, sout_ref, st_ref, *, chunk, prec):
    tb = pl.program_id(2)
    rows_per_step = q_ref.shape[0]
    n_chunks = rows_per_step // chunk
    nb = chunk // SUBLANES

    @pl.when(tb == 0)
    def _():
        st_ref[...] = s0_ref[...].T

    row = lax.broadcasted_iota(jnp.int32, (chunk, GLA_DK), 0)
    rr = lax.broadcasted_iota(jnp.int32, (chunk, chunk), 0)
    cc = lax.broadcasted_iota(jnp.int32, (chunk, chunk), 1)
    tril = (cc <= rr).astype(F32)
    levels = []
    m = chunk // 2
    while m >= SUBLANES:
        levels.append(m)
        m //= 2
    wup = wup_ref[...]
    bg = bg_ref[...]
    gout = gout_ref[...]
    scale = GLA_DK ** -0.5

    def chunk_body(c, carry):
        r0 = pl.multiple_of(c * chunk, chunk)
        rows = pl.ds(r0, chunk)
        q = q_ref[rows, :].astype(F32) * scale
        k = k_ref[rows, :].astype(F32)
        v = v_ref[rows, :].astype(F32)
        z = _mm(glr_ref[rows, :], wup, prec) + bg
        la = (jnp.minimum(z, 0.0) - jnp.log1p(jnp.exp(-jnp.abs(z)))) * (1.0 / GLA_TAU)
        b = _mm(tril, la, 'f32')
        st = st_ref[...]
        o = _mm(q * jnp.exp(b), st, prec, trans_b=True)

        a = jnp.zeros((chunk, chunk), F32)
        for m in levels:
            mb = m // SUBLANES
            sh = m.bit_length() - 1
            q_idx = [max((r // mb) * m - 1, 0) for r in range(nb)]
            k_idx = [(r // mb + 1) * m - 1 for r in range(nb)]
            q_role = ((row >> sh) & 1) == 1
            qt = q * jnp.exp(jnp.where(q_role, b - _bcast_rows(b, q_idx), NEG))
            kt = k * jnp.exp(jnp.where(q_role, NEG, _bcast_rows(b, k_idx) - b))
            a_l = _mm(qt, kt, 'f32', trans_b=True)
            a = a + jnp.where((cc >> sh) == (rr >> sh) - 1, a_l, 0.0)
        tl = row & (SUBLANES - 1)
        for j in range(SUBLANES):
            idx = [SUBLANES * r + j for r in range(nb)]
            e = jnp.exp(jnp.where(tl >= j, b - _bcast_rows(b, idx), NEG))
            w = jnp.sum(q * e * _bcast_rows(k, idx), axis=-1, keepdims=True)
            a = a + jnp.where(cc == (rr & -SUBLANES) + j, w, 0.0)
        o = o + _mm(a, v, prec)

        b_end = b[chunk - 1:chunk, :]
        khat = k * jnp.exp(b_end - b)
        dn = (((0,), (0,)), ((), ()))
        if prec == 'bf16':
            upd = lax.dot_general(v.astype(BF16), khat.astype(BF16), dn, preferred_element_type=F32)
        else:
            upd = lax.dot_general(v, khat, dn, precision=HIGHEST, preferred_element_type=F32)
        st_ref[...] = st * jnp.exp(b_end) + upd

        on = o * lax.rsqrt(jnp.mean(o * o, axis=-1, keepdims=True) + EPS) * gout
        r = r_ref[rows, :].astype(F32)
        o_ref[rows, :] = (on * (r * _sigmoid(r))).astype(o_ref.dtype)
        return carry

    lax.fori_loop(0, n_chunks, chunk_body, 0, unroll=min(2, n_chunks))

    @pl.when(tb == pl.num_programs(2) - 1)
    def _():
        sout_ref[...] = st_ref[...].T


def gla(pg, pglr, wup_pad, b_gate, g_out, s0, layer, *, batch, seq, chunk, rows_per_step,
        row0, prec, out_dtype):
    n = batch * seq
    steps = seq // rows_per_step
    kq = GLA_HEADS * GLA_DK // GLA_DK
    kv = 2 * GLA_HEADS * GLA_DK // GLA_DV
    kr = kv + GLA_HEADS
    blk0 = row0 // rows_per_step
    row_map = lambda b, h, t: blk0 + b * steps + t
    out_map = lambda b, h, t: b * steps + t
    bg3 = b_gate.reshape(b_gate.shape[0], 1, -1)
    go3 = g_out.reshape(g_out.shape[0], 1, -1)
    if s0.ndim == 4:
        s0_spec = pl.BlockSpec((None, None, GLA_DK, GLA_DV), lambda b, h, t: (b, h, 0, 0))
    else:
        s0_spec = pl.BlockSpec((None, None, None, GLA_DK, GLA_DV), lambda b, h, t: (layer, b, h, 0, 0))
    return pl.pallas_call(
        functools.partial(_gla_kernel, chunk=chunk, prec=prec),
        grid=(batch, GLA_HEADS, steps),
        in_specs=[
            pl.BlockSpec((rows_per_step, GLA_DK), lambda b, h, t: (row_map(b, h, t), h)),
            pl.BlockSpec((rows_per_step, GLA_DK), lambda b, h, t: (row_map(b, h, t), kq + h)),
            pl.BlockSpec((rows_per_step, GLA_DV), lambda b, h, t: (row_map(b, h, t), kv + h)),
            pl.BlockSpec((rows_per_step, GLA_DV), lambda b, h, t: (row_map(b, h, t), kr + h)),
            pl.BlockSpec((rows_per_step, LANES), lambda b, h, t: (row_map(b, h, t), 0)),
            pl.BlockSpec((None, LANES, GLA_DK), lambda b, h, t: (layer, 0, h)),
            pl.BlockSpec((None, 1, GLA_DK), lambda b, h, t: (layer, 0, h)),
            pl.BlockSpec((None, 1, GLA_DV), lambda b, h, t: (layer, 0, 0)),
            s0_spec,
        ],
        out_specs=[
            pl.BlockSpec((rows_per_step, GLA_DV), lambda b, h, t: (out_map(b, h, t), h)),
            pl.BlockSpec((None, None, GLA_DK, GLA_DV), lambda b, h, t: (b, h, 0, 0)),
        ],
        out_shape=[jax.ShapeDtypeStruct((n, GLA_HEADS * GLA_DV), out_dtype),
                   jax.ShapeDtypeStruct((batch, GLA_HEADS, GLA_DK, GLA_DV), F32)],
        scratch_shapes=[pltpu.VMEM((GLA_DV, GLA_DK), F32)],
        compiler_params=_cparams(("parallel", "parallel", "arbitrary")),
        name="gla",
    )(pg, pg, pg, pg, pglr, wup_pad, bg3, go3, s0)


def _head_norm(x, g):
    return x * lax.rsqrt(jnp.mean(x * x, axis=-1, keepdims=True) + EPS) * g


def _attn_prompt_kernel(*refs, seq, prec):
    (q0, q1, q2, k0, k1, k2, v0, v1, v2, gq_ref, gk_ref,
     att_ref, kc0, vc0, kc1, vc1, kc2, vc2, qn_s, kn_s, o_s, l_s) = refs
    q_refs, k_refs, v_refs = (q0, q1, q2), (k0, k1, k2), (v0, v1, v2)
    kc_refs, vc_refs = (kc0, kc1, kc2), (vc0, vc1, vc2)
    gq = gq_ref[...]
    gk = gk_ref[...]
    qb_rows = Q_BLOCK
    ri = lax.broadcasted_iota(jnp.int32, (qb_rows, qb_rows), 0)
    ci = lax.broadcasted_iota(jnp.int32, (qb_rows, qb_rows), 1)
    cur_ok = ci <= ri
    prev_ok = ci >= ri

    for g in range(3):
        w = min(WINDOWS[g], seq)
        qn_s[g] = _head_norm(q_refs[g][...].astype(F32), gq) * (HEAD_DIM ** -0.5)
        kn = _head_norm(k_refs[g][...].astype(F32), gk)
        kn_s[g] = kn
        kc_refs[g][...] = kn[seq - w:, :]
        vc_refs[g][...] = v_refs[g][seq - w:, :].astype(F32)

    for g in range(3):
        d = DILATIONS[g]
        assert WINDOWS[g] // d == qb_rows
        sub_len = seq // d
        nqb = sub_len // qb_rows
        qn_g, kn_g, v_g, o_g, l_g = qn_s.at[g], kn_s.at[g], v_refs[g], o_s.at[g], l_s.at[g]

        def block(i, carry, d=d, nqb=nqb, qn_g=qn_g, kn_g=kn_g, v_g=v_g, o_g=o_g, l_g=l_g):
            res = i // nqb
            qb = i % nqb
            cur = pl.ds(res + d * qb_rows * qb, qb_rows, stride=d)
            q = qn_g[cur, :]
            kc = kn_g[cur, :]
            vc = v_g[cur, :].astype(F32)
            s_cur = jnp.where(cur_ok, _mm(q, kc, prec, trans_b=True), NEG)
            m = jnp.max(s_cur, axis=-1, keepdims=True)
            if nqb > 1:
                pb = jnp.maximum(qb - 1, 0)
                prev = pl.ds(res + d * qb_rows * pb, qb_rows, stride=d)
                kp = kn_g[prev, :]
                vp = v_g[prev, :].astype(F32)
                s_prev = jnp.where(prev_ok, _mm(q, kp, prec, trans_b=True), NEG)
                s_prev = s_prev + jnp.where(qb > 0, 0.0, NEG)
                m = jnp.maximum(m, jnp.max(s_prev, axis=-1, keepdims=True))
            p_cur = jnp.exp(s_cur - m)
            den = jnp.sum(p_cur, axis=-1, keepdims=True)
            if nqb > 1:
                p_prev = jnp.exp(s_prev - m)
                den = den + jnp.sum(p_prev, axis=-1, keepdims=True)
            acc = _mm(p_cur / den, vc, prec)
            if nqb > 1:
                acc = acc + _mm(p_prev / den, vp, prec)
            o_g[cur, :] = acc
            l_g[cur, :] = jnp.broadcast_to(m + jnp.log(den), (qb_rows, HEAD_DIM))
            return carry

        lax.fori_loop(0, d * nqb, block, 0, unroll=2)

    def combine(i, carry):
        rows = pl.ds(pl.multiple_of(i * qb_rows, qb_rows), qb_rows)
        l0, l1, l2 = l_s[0, rows, :], l_s[1, rows, :], l_s[2, rows, :]
        m = jnp.maximum(jnp.maximum(l0, l1), l2)
        e0, e1, e2 = jnp.exp(l0 - m), jnp.exp(l1 - m), jnp.exp(l2 - m)
        tot = e0 + e1 + e2
        out = o_s[0, rows, :] * (e0 / tot) + o_s[1, rows, :] * (e1 / tot) + o_s[2, rows, :] * (e2 / tot)
        att_ref[rows, :] = out.astype(att_ref.dtype)
        return carry

    lax.fori_loop(0, seq // qb_rows, combine, 0)


def attn_prompt(pa, g_q, g_k, layer, *, batch, seq, prec, out_dtype):
    hpg = HEADS_PER_GROUP
    nh = 3 * hpg
    gq3 = g_q.reshape(g_q.shape[0], 1, -1)
    gk3 = g_k.reshape(g_k.shape[0], 1, -1)
    in_specs = []
    for part in range(3):
        for g in range(3):
            in_specs.append(pl.BlockSpec(
                (seq, HEAD_DIM), lambda b, j, part=part, g=g: (b, part * nh + g * hpg + j)))
    in_specs += [pl.BlockSpec((None, 1, HEAD_DIM), lambda b, j: (layer, 0, 0))] * 2
    out_specs = [pl.BlockSpec((seq, HEAD_DIM), lambda b, j: (b, j))]
    out_shape = [jax.ShapeDtypeStruct((batch * seq, hpg * HEAD_DIM), out_dtype)]
    for g in range(3):
        w = min(WINDOWS[g], seq)
        for _ in range(2):
            out_specs.append(pl.BlockSpec((None, w, HEAD_DIM), lambda b, j: (b, 0, j)))
            out_shape.append(jax.ShapeDtypeStruct((batch, w, hpg * HEAD_DIM), F32))
    res = pl.pallas_call(
        functools.partial(_attn_prompt_kernel, seq=seq, prec=prec),
        grid=(batch, hpg),
        in_specs=in_specs,
        out_specs=out_specs,
        out_shape=out_shape,
        scratch_shapes=[pltpu.VMEM((3, seq, HEAD_DIM), F32)] * 4,
        compiler_params=_cparams(("parallel", "parallel")),
        name="attn_prompt",
    )(*([pa] * 9), gq3, gk3)
    return res[0], [(res[1 + 2 * g], res[2 + 2 * g]) for g in range(3)]


def _attn_sample_kernel(*refs, n_new, prec):
    (q0, q1, q2, k0, k1, k2, v0, v1, v2, c0, c1, c2, gq_ref, gk_ref,
     att_ref, kn0, vn0, kn1, vn1, kn2, vn2) = refs
    q_refs, k_refs, v_refs = (q0, q1, q2), (k0, k1, k2), (v0, v1, v2)
    c_refs = (c0, c1, c2)
    kn_refs, vn_refs = (kn0, kn1, kn2), (vn0, vn1, vn2)
    hpg = HEADS_PER_GROUP
    head = pl.program_id(1)
    gq = gq_ref[...]
    gk = gk_ref[...]
    rnd = _round_bf16 if prec == 'bf16' else (lambda a: a)
    outs, lses = [], []
    for g in range(3):
        d = DILATIONS[g]
        nk = WINDOWS[g] // d
        clen = c_refs[g].shape[0] // (2 * hpg)
        ck = c_refs[g][pl.ds(head, clen, stride=2 * hpg), :]
        cv = c_refs[g][pl.ds(hpg + head, clen, stride=2 * hpg), :]
        q = _head_norm(q_refs[g][...], gq) * (HEAD_DIM ** -0.5)
        kn = _head_norm(k_refs[g][...], gk)
        vn = v_refs[g][...]
        kn_refs[g][...] = kn
        vn_refs[g][...] = vn
        s_c = _mm(q, ck, prec, trans_b=True)
        t_c = lax.broadcasted_iota(jnp.int32, (n_new, clen), 0)
        n_c = lax.broadcasted_iota(jnp.int32, (n_new, clen), 1)
        off = clen + t_c - n_c
        s_c = jnp.where(((off & (d - 1)) == 0) & (off <= nk * d), s_c, NEG)
        t_n = lax.broadcasted_iota(jnp.int32, (n_new, n_new), 0)
        n_n = lax.broadcasted_iota(jnp.int32, (n_new, n_new), 1)
        s_n = jnp.zeros((n_new, n_new), F32)
        q_r, kn_r, vn_r = rnd(q), rnd(kn), rnd(vn)
        for j in range(n_new):
            col = jnp.sum(q_r * kn_r[j:j + 1, :], axis=-1, keepdims=True)
            s_n = s_n + jnp.where(n_n == j, col, 0.0)
        off_n = t_n - n_n
        s_n = jnp.where((off_n >= 0) & ((off_n & (d - 1)) == 0) & (off_n <= nk * d), s_n, NEG)
        m = jnp.maximum(jnp.max(s_c, axis=-1, keepdims=True), jnp.max(s_n, axis=-1, keepdims=True))
        p_c = jnp.exp(s_c - m)
        p_n = jnp.exp(s_n - m)
        den = jnp.sum(p_c, axis=-1, keepdims=True) + jnp.sum(p_n, axis=-1, keepdims=True)
        acc = _mm(p_c / den, cv, prec)
        pn_r = rnd(p_n / den)
        for j in range(n_new):
            acc = acc + pn_r[:, j:j + 1] * vn_r[j:j + 1, :]
        outs.append(acc)
        lses.append(m + jnp.log(den))
    m = jnp.maximum(jnp.maximum(lses[0], lses[1]), lses[2])
    es = [jnp.exp(l - m) for l in lses]
    tot = es[0] + es[1] + es[2]
    att_ref[...] = (outs[0] * (es[0] / tot) + outs[1] * (es[1] / tot)
                    + outs[2] * (es[2] / tot)).astype(att_ref.dtype)


def attn_sample(pa, caches, g_q, g_k, layer, *, batch, n_new, row0, prec, out_dtype):
    hpg = HEADS_PER_GROUP
    nh = 3 * hpg
    gq3 = g_q.reshape(g_q.shape[0], 1, -1)
    gk3 = g_k.reshape(g_k.shape[0], 1, -1)
    blk0 = row0 // n_new
    in_specs = []
    for part in range(3):
        for g in range(3):
            in_specs.append(pl.BlockSpec(
                (n_new, HEAD_DIM), lambda b, j, part=part, g=g: (blk0 + b, part * nh + g * hpg + j)))
    cache_args = []
    for g in range(3):
        c = caches[g]
        clen = c.shape[2]
        in_specs.append(pl.BlockSpec((None, None, clen * 2 * hpg, HEAD_DIM), lambda b, j: (layer, b, 0, 0)))
        cache_args.append(c.reshape(c.shape[0], c.shape[1], clen * 2 * hpg, HEAD_DIM))
    in_specs += [pl.BlockSpec((None, 1, HEAD_DIM), lambda b, j: (layer, 0, 0))] * 2
    out_specs = [pl.BlockSpec((n_new, HEAD_DIM), lambda b, j: (b, j))]
    out_shape = [jax.ShapeDtypeStruct((batch * n_new, hpg * HEAD_DIM), out_dtype)]
    for g in range(3):
        for _ in range(2):
            out_specs.append(pl.BlockSpec((n_new, HEAD_DIM), lambda b, j: (b, j)))
            out_shape.append(jax.ShapeDtypeStruct((batch * n_new, hpg * HEAD_DIM), F32))
    res = pl.pallas_call(
        functools.partial(_attn_sample_kernel, n_new=n_new, prec=prec),
        grid=(batch, hpg),
        in_specs=in_specs,
        out_specs=out_specs,
        out_shape=out_shape,
        compiler_params=_cparams(("parallel", "parallel")),
        name="attn_sample",
    )(*([pa] * 9), *cache_args, gq3, gk3)
    return res[0], [(res[1 + 2 * g], res[2 + 2 * g]) for g in range(3)]


def _merge_kernel(h_ref, og_ref, at_ref, wg1_ref, wg2_ref, wgp_ref, wap_ref, o_ref, *, prec):
    h = h_ref[...]
    g1 = _mm(h, wg1_ref[...], prec)
    g2 = _mm(h, wg2_ref[...], prec)
    a = _mm(og_ref[...], wgp_ref[...], prec)
    b = _mm(at_ref[...], wap_ref[...], prec)
    o_ref[...] = (_sigmoid(g1) * a + _sigmoid(g2) * b).astype(o_ref.dtype)


def merge(h, og, att, wmg, wgp, wap, layer, *, prec, out_dtype, tm, tn):
    m, d = h.shape
    nj = d // tn
    return pl.pallas_call(
        functools.partial(_merge_kernel, prec=prec),
        grid=(m // tm, nj),
        in_specs=[
            pl.BlockSpec((tm, d), lambda i, j: (i, 0)),
            pl.BlockSpec((tm, og.shape[1]), lambda i, j: (i, 0)),
            pl.BlockSpec((tm, att.shape[1]), lambda i, j: (i, 0)),
            _wspec(wmg, layer, (d, tn), lambda i, j: (0, j)),
            _wspec(wmg, layer, (d, tn), lambda i, j: (0, nj + j)),
            _wspec(wgp, layer, (og.shape[1], tn), lambda i, j: (0, j)),
            _wspec(wap, layer, (att.shape[1], tn), lambda i, j: (0, j)),
        ],
        out_specs=pl.BlockSpec((tm, tn), lambda i, j: (i, j)),
        out_shape=jax.ShapeDtypeStruct((m, d), out_dtype),
        compiler_params=_cparams(("parallel", "arbitrary")),
        name="merge",
    )(h, og, att, wmg, wmg, wgp, wap)


def _router_kernel(x_ref, g_ref, wr_ref, h_ref, route_ref, cnt_ref, carry_ref):
    @pl.when(pl.program_id(0) == 0)
    def _():
        carry_ref[...] = jnp.zeros_like(carry_ref)

    x = x_ref[...]
    h = x * lax.rsqrt(jnp.mean(x * x, axis=-1, keepdims=True) + EPS) * g_ref[...]
    h_ref[...] = h.astype(h_ref.dtype)
    logits = _mm(h, wr_ref[...], 'bf16')
    lane = lax.broadcasted_iota(jnp.int32, logits.shape, 1)
    lane_f = lane.astype(F32)
    big = float(LANES)
    is_g = lane < N_GROUPS
    gl = jnp.where(is_g, logits, NEG)
    gmax = jnp.max(gl, axis=-1, keepdims=True)
    gsel = jnp.min(jnp.where(gl == gmax, lane_f, big), axis=-1, keepdims=True)
    gprob = 1.0 / jnp.sum(jnp.where(is_g, jnp.exp(gl - gmax), 0.0), axis=-1, keepdims=True)
    e_idx = lane - N_GROUPS
    e_grp = e_idx >> (EXPERTS_PER_GROUP.bit_length() - 1)
    in_grp = (e_idx >= 0) & (e_idx < N_EXPERTS) & (e_grp.astype(F32) == gsel)
    el = jnp.where(in_grp, logits, NEG)
    v1 = jnp.max(el, axis=-1, keepdims=True)
    i1 = jnp.min(jnp.where(el == v1, lane_f, big), axis=-1, keepdims=True)
    el2 = jnp.where(lane_f == i1, NEG, el)
    v2 = jnp.max(el2, axis=-1, keepdims=True)
    i2 = jnp.min(jnp.where(el2 == v2, lane_f, big), axis=-1, keepdims=True)
    t = jnp.exp(v2 - v1)
    w1 = gprob / (1.0 + t)
    w2 = gprob * (t / (1.0 + t))
    tm = x.shape[0]
    picked = ((lane_f == i1) | (lane_f == i2)).astype(F32)
    rr = lax.broadcasted_iota(jnp.int32, (tm, tm), 0)
    cc = lax.broadcasted_iota(jnp.int32, (tm, tm), 1)
    before = _mm((cc < rr).astype(F32), picked, 'bf16') + carry_ref[...]
    r1 = jnp.sum(jnp.where(lane_f == i1, before, 0.0), axis=-1, keepdims=True)
    r2 = jnp.sum(jnp.where(lane_f == i2, before, 0.0), axis=-1, keepdims=True)
    carry_ref[...] = carry_ref[...] + jnp.sum(picked, axis=0, keepdims=True)
    cnt_ref[...] = carry_ref[...]
    vals = (i1 - N_GROUPS, i2 - N_GROUPS, w1, w2, r1, r2)
    route = jnp.zeros_like(logits)
    for k, v in enumerate(vals):
        route = jnp.where(lane == k, v, route)
    route_ref[...] = route


def router(x, g, wr, layer, *, tm):
    m, d = x.shape
    g3 = g.reshape(g.shape[0], 1, d)
    return pl.pallas_call(
        _router_kernel,
        grid=(m // tm,),
        in_specs=[pl.BlockSpec((tm, d), lambda i: (i, 0)),
                  pl.BlockSpec((None, 1, d), lambda i: (layer, 0, 0)),
                  pl.BlockSpec((None, d, LANES), lambda i: (layer, 0, 0))],
        out_specs=[pl.BlockSpec((tm, d), lambda i: (i, 0)),
                   pl.BlockSpec((tm, LANES), lambda i: (i, 0)),
                   pl.BlockSpec((1, LANES), lambda i: (0, 0))],
        out_shape=[jax.ShapeDtypeStruct((m, d), F32),
                   jax.ShapeDtypeStruct((m, LANES), F32),
                   jax.ShapeDtypeStruct((1, LANES), F32)],
        scratch_shapes=[pltpu.VMEM((1, LANES), F32)],
        compiler_params=_cparams(("arbitrary",)),
        name="router",
    )(x, g3, wr)


def _row_copy(src_hbm, dst, sem, src_row, dst_row):
    return pltpu.make_async_copy(src_hbm.at[pl.ds(src_row, 1)], dst.at[pl.ds(dst_row, 1)], sem)


def _row_source_kernel(pos_ref, out_ref):
    def clear(r, c):
        out_ref[r] = 0
        return c

    def put(i, c):
        out_ref[pos_ref[i]] = i // TOP_K
        return c

    lax.fori_loop(0, out_ref.shape[0], clear, 0, unroll=8)
    lax.fori_loop(0, pos_ref.shape[0], put, 0, unroll=8)


def row_source(pos, *, rows):
    return pl.pallas_call(
        _row_source_kernel,
        grid_spec=pltpu.PrefetchScalarGridSpec(
            num_scalar_prefetch=1,
            grid=(1,),
            in_specs=[],
            out_specs=pl.BlockSpec(memory_space=pltpu.SMEM),
        ),
        out_shape=jax.ShapeDtypeStruct((rows,), jnp.int32),
        compiler_params=_cparams(("arbitrary",)),
        name="moe_row_source",
    )(pos)


def _tile_row_dmas(src_hbm, dst, sem, n_rows, src_row_of):
    def run(method):
        def body(i, c):
            getattr(_row_copy(src_hbm, dst, sem, src_row_of(i), i), method)()
            return c
        lax.fori_loop(0, n_rows, body, 0, unroll=8)
    return (lambda: run("start")), (lambda: run("wait"))


def _gather_rows_kernel(idx_ref, src_hbm, o_ref, buf, sem):
    rb = o_ref.shape[0]
    step = pl.program_id(0)
    slot = step % 2

    def dmas(tile, s):
        return _tile_row_dmas(src_hbm, buf.at[s], sem.at[s], rb, lambda i: idx_ref[tile * rb + i])

    @pl.when(step == 0)
    def _():
        dmas(0, 0)[0]()

    @pl.when(step + 1 < pl.num_programs(0))
    def _():
        dmas(step + 1, 1 - slot)[0]()

    dmas(step, slot)[1]()
    o_ref[...] = buf[slot]


def gather_rows(src, idx, *, rb):
    rows = idx.shape[0]
    d = src.shape[1]
    return pl.pallas_call(
        _gather_rows_kernel,
        grid_spec=pltpu.PrefetchScalarGridSpec(
            num_scalar_prefetch=1,
            grid=(rows // rb,),
            in_specs=[pl.BlockSpec(memory_space=pl.ANY)],
            out_specs=pl.BlockSpec((rb, d), lambda i, idx: (i, 0)),
            scratch_shapes=[pltpu.VMEM((2, rb, d), src.dtype), pltpu.SemaphoreType.DMA((2,))],
        ),
        out_shape=jax.ShapeDtypeStruct((rows, d), src.dtype),
        compiler_params=_cparams(("arbitrary",)),
        name="moe_gather",
    )(idx, src)


def _moe_up_kernel(blk_e_ref, nused_ref, x_ref, wg_ref, wu_ref, h_ref, *, prec):
    used = pl.program_id(1) < nused_ref[0]

    @pl.when(used)
    def _():
        x = x_ref[...]
        g = _mm(x, wg_ref[...], prec)
        u = _mm(x, wu_ref[...], prec)
        h_ref[...] = (g * _sigmoid(g) * u).astype(h_ref.dtype)

    @pl.when(jnp.logical_not(used))
    def _():
        h_ref[...] = jnp.zeros_like(h_ref)


def _moe_down_kernel(blk_e_ref, nused_ref, h_ref, wd_ref, y_ref, *, prec):
    used = pl.program_id(1) < nused_ref[0]

    @pl.when(used)
    def _():
        y_ref[...] = _mm(h_ref[...], wd_ref[...], prec)

    @pl.when(jnp.logical_not(used))
    def _():
        y_ref[...] = jnp.zeros_like(y_ref)


def moe_experts(xs, blk_e, n_used, w_gate, w_up, w_down, layer, *, rb, tf, tn, prec, h_dtype):
    rows, d = xs.shape
    f = w_gate.shape[-1]
    nb = rows // rb
    hmid = pl.pallas_call(
        functools.partial(_moe_up_kernel, prec=prec),
        grid_spec=pltpu.PrefetchScalarGridSpec(
            num_scalar_prefetch=2,
            grid=(f // tf, nb),
            in_specs=[
                pl.BlockSpec((rb, d), lambda j, i, be, nu: (jnp.minimum(i, nu[0] - 1), 0)),
                pl.BlockSpec((None, None, d, tf), lambda j, i, be, nu: (layer, be[i], 0, j)),
                pl.BlockSpec((None, None, d, tf), lambda j, i, be, nu: (layer, be[i], 0, j)),
            ],
            out_specs=pl.BlockSpec((rb, tf), lambda j, i, be, nu: (i, j)),
        ),
        out_shape=jax.ShapeDtypeStruct((rows, f), h_dtype),
        compiler_params=_cparams(("arbitrary", "arbitrary")),
        name="moe_up",
    )(blk_e, n_used, xs, w_gate, w_up)
    return pl.pallas_call(
        functools.partial(_moe_down_kernel, prec=prec),
        grid_spec=pltpu.PrefetchScalarGridSpec(
            num_scalar_prefetch=2,
            grid=(d // tn, nb),
            in_specs=[
                pl.BlockSpec((rb, f), lambda j, i, be, nu: (i, 0)),
                pl.BlockSpec((None, None, f, tn), lambda j, i, be, nu: (layer, be[i], 0, j)),
            ],
            out_specs=pl.BlockSpec((rb, tn), lambda j, i, be, nu: (i, j)),
        ),
        out_shape=jax.ShapeDtypeStruct((rows, d), F32),
        compiler_params=_cparams(("arbitrary", "arbitrary")),
        name="moe_down",
    )(blk_e, n_used, hmid, w_down)


def _combine_kernel(pos_ref, x_ref, route_ref, y_hbm, o_ref, buf, sem):
    tm = x_ref.shape[0]
    step = pl.program_id(0)
    slot = step % 2

    def dmas(tile, s, k):
        return _tile_row_dmas(y_hbm, buf.at[s, k], sem.at[s], tm,
                              lambda i: pos_ref[TOP_K * (tile * tm + i) + k])

    @pl.when(step == 0)
    def _():
        for k in range(TOP_K):
            dmas(0, 0, k)[0]()

    @pl.when(step + 1 < pl.num_programs(0))
    def _():
        for k in range(TOP_K):
            dmas(step + 1, 1 - slot, k)[0]()

    for k in range(TOP_K):
        dmas(step, slot, k)[1]()
    route = route_ref[...]
    g1 = route[:, TOP_K:TOP_K + 1]
    g2 = route[:, TOP_K + 1:TOP_K + 2]
    o_ref[...] = x_ref[...] + (buf[slot, 0] * g1 + buf[slot, 1] * g2)


def moe_combine(x, y, route, pos, *, tm):
    m, d = x.shape
    return pl.pallas_call(
        _combine_kernel,
        grid_spec=pltpu.PrefetchScalarGridSpec(
            num_scalar_prefetch=1,
            grid=(m // tm,),
            in_specs=[pl.BlockSpec((tm, d), lambda i, pos: (i, 0)),
                      pl.BlockSpec((tm, LANES), lambda i, pos: (i, 0)),
                      pl.BlockSpec(memory_space=pl.ANY)],
            out_specs=pl.BlockSpec((tm, d), lambda i, pos: (i, 0)),
            scratch_shapes=[pltpu.VMEM((2, TOP_K, tm, d), F32), pltpu.SemaphoreType.DMA((2,))],
        ),
        out_shape=jax.ShapeDtypeStruct((m, d), F32),
        compiler_params=_cparams(("arbitrary",)),
        name="moe_combine",
    )(pos, x, route, y)


def _dispatch_plan(route, counts, rb):
    n = route.shape[0]
    experts = jnp.arange(N_EXPERTS, dtype=jnp.int32)
    eid = route[:, :TOP_K].astype(jnp.int32)
    rank = route[:, 2 * TOP_K:3 * TOP_K].astype(jnp.int32)
    counts = counts.astype(jnp.int32)
    padded = (counts + rb - 1) // rb * rb
    pend = jnp.cumsum(padded)
    pstarts = pend - padded
    npair = n * TOP_K
    nb = (npair + N_EXPERTS * (rb - 1) + rb - 1) // rb
    blk_start = jnp.arange(nb, dtype=jnp.int32) * rb
    blk_e = jnp.minimum(jnp.sum(pend[None, :] <= blk_start[:, None], axis=1), N_EXPERTS - 1)
    start_of = jnp.sum(jnp.where(eid[..., None] == experts, pstarts, 0), axis=-1)
    pos = (start_of + rank).reshape(npair).astype(jnp.int32)
    n_used = (pend[-1:] // rb).astype(jnp.int32)
    return pos, blk_e.astype(jnp.int32), n_used, nb * rb


def moe_layer(x, g, wr, w_gate, w_up, w_down, layer, *, prec, rb, tf, tn, tm_router, tm_combine):
    h2, route, cnt = router(x, g, wr, layer, tm=tm_router)
    counts = cnt[0, N_GROUPS:N_GROUPS + N_EXPERTS]
    pos, blk_e, n_used, rows = _dispatch_plan(route, counts, rb)
    xs = gather_rows(h2, row_source(pos, rows=rows), rb=rb)
    h_dtype = BF16 if prec == 'bf16' else F32
    y = moe_experts(xs, blk_e, n_used, w_gate, w_up, w_down, layer,
                    rb=rb, tf=tf, tn=tn, prec=prec, h_dtype=h_dtype)
    return moe_combine(x, y, route, pos, tm=tm_combine)


GLA_COLS = 2 * GLA_HEADS * GLA_DK + 2 * GLA_HEADS * GLA_DV
ATT_COLS = 3 * 3 * HEADS_PER_GROUP * HEAD_DIM


PREC = 'bf16'
MOE_ROW_BLOCK = 128
BF16_ROWS = 16


def _row_tile(n, target):
    best = None
    for t in range(BF16_ROWS, target + 1, BF16_ROWS):
        if n % t == 0:
            best = t
    assert best is not None, (n, target)
    return best


def kernel(x_prompt, x_sample, cache_kv_w128, cache_kv_w512, cache_kv_w2048, state_gla,
           g_norm_mix, w_in, w_gla_gate_up, b_gla_gate, g_q, g_k, g_gla_out,
           w_gla_proj, w_att_proj, w_merge_gate, w_out, g_norm_ffn,
           w_router_group, w_router_expert, w_exp_gate, w_exp_up, w_exp_down):
    caches = (cache_kv_w128, cache_kv_w512, cache_kv_w2048)
    bp, sp, d = x_prompt.shape
    bs, ss, _ = x_sample.shape
    depth = w_in.shape[0]
    hpg = HEADS_PER_GROUP
    n_p, n_s = bp * sp, bs * ss
    n = n_p + n_s
    x = jnp.concatenate([x_prompt.reshape(n_p, d), x_sample.reshape(n_s, d)], axis=0)
    tm_norm = _row_tile(n, 256)
    tm_mm = _row_tile(n, 1536)
    tm_merge = _row_tile(n, 768)
    tm_combine = _row_tile(n, 128)

    glr0 = GLA_COLS
    att0 = GLA_COLS + GLA_RANK
    wup_pad = jnp.pad(w_gla_gate_up, ((0, 0), (0, LANES - GLA_RANK), (0, 0)))
    wr = jnp.pad(jnp.concatenate([w_router_group, w_router_expert], axis=-1),
                 ((0, 0), (0, 0), (0, LANES - N_GROUPS - N_EXPERTS)))
    zero_state = jnp.zeros((bp, GLA_HEADS, GLA_DK, GLA_DV), F32)

    kv_p = [[] for _ in range(3)]
    kv_s = [[] for _ in range(3)]
    st_p, st_s = [], []
    for l in range(depth):
        wglr = jnp.pad(w_in[l, :, glr0:att0], ((0, 0), (0, LANES - GLA_RANK))).astype(BF16)
        w3 = w_in[l, :, att0:].astype(BF16)
        wmg = w_merge_gate[l].astype(BF16)
        wgp = w_gla_proj[l].astype(BF16)
        wap = w_att_proj[l].astype(BF16)

        h = rmsnorm(x, g_norm_mix, l, out_dtype=BF16, tm=tm_norm)
        pg = matmul(h, w_in, prec=PREC, out_dtype=F32, tm=tm_mm, tn=256, layer=l, ncols=GLA_COLS)
        pglr = matmul(h, wglr, prec=PREC, out_dtype=F32, tm=tm_mm, tn=LANES)
        pa = matmul(h, w3, prec=PREC, out_dtype=F32, tm=tm_mm, tn=512)

        og_p, st = gla(pg, pglr, wup_pad, b_gla_gate, g_gla_out, zero_state, l, batch=bp, seq=sp,
                       chunk=64, rows_per_step=512, row0=0, prec=PREC, out_dtype=BF16)
        st_p.append(st)
        og_s, st = gla(pg, pglr, wup_pad, b_gla_gate, g_gla_out, state_gla, l, batch=bs, seq=ss,
                       chunk=ss, rows_per_step=ss, row0=n_p, prec=PREC, out_dtype=F32)
        st_s.append(st)

        att_p, kvs = attn_prompt(pa, g_q, g_k, l, batch=bp, seq=sp, prec=PREC, out_dtype=BF16)
        for g in range(3):
            w = min(WINDOWS[g], sp)
            kv_p[g].append(jnp.stack([kvs[g][0].reshape(bp, w, hpg, HEAD_DIM),
                                      kvs[g][1].reshape(bp, w, hpg, HEAD_DIM)], axis=2))
        att_s, kvs = attn_sample(pa, caches, g_q, g_k, l, batch=bs, n_new=ss, row0=n_p,
                                 prec=PREC, out_dtype=F32)
        for g in range(3):
            kv_s[g].append(jnp.stack([kvs[g][0].reshape(bs, ss, hpg, HEAD_DIM),
                                      kvs[g][1].reshape(bs, ss, hpg, HEAD_DIM)], axis=2))

        og = jnp.concatenate([og_p, og_s.astype(BF16)], axis=0)
        att = jnp.concatenate([att_p, att_s.astype(BF16)], axis=0)
        mixed = merge(h, og, att, wmg, wgp, wap, None, prec=PREC, out_dtype=BF16, tm=tm_merge, tn=256)
        x = matmul(mixed, w_out, prec=PREC, out_dtype=F32, tm=tm_mm, tn=256, layer=l, residual=x)
        x = moe_layer(x, g_norm_ffn, wr, w_exp_gate, w_exp_up, w_exp_down, l, prec=PREC,
                      rb=MOE_ROW_BLOCK, tf=512, tn=2048, tm_router=tm_norm, tm_combine=tm_combine)

    xp, xs = x[:n_p], x[n_p:]
    return (xp.reshape(bp, sp, d), xs.reshape(bs, ss, d),
            jnp.stack(kv_p[0]), jnp.stack(kv_s[0]),
            jnp.stack(kv_p[1]), jnp.stack(kv_s[1]),
            jnp.stack(kv_p[2]), jnp.stack(kv_s[2]),
            jnp.stack(st_p), jnp.stack(st_s))
```

```python
import functools

import jax
import jax.numpy as jnp
from jax import lax
from jax.experimental import pallas as pl
from jax.experimental.pallas import tpu as pltpu

F32 = jnp.float32
BF16 = jnp.bfloat16
HIGHEST = lax.Precision.HIGHEST

LANES = 128
SUBLANES = 8
VMEM_LIMIT_BYTES = 56 * 1024 * 1024

EPS = 1e-6
NEG = -1e30

GLA_HEADS = 8
GLA_DK = 128
GLA_DV = 256
GLA_TAU = 16.0
GLA_RANK = 16
HEAD_DIM = 128
HEADS_PER_GROUP = 8
WINDOWS = (128, 512, 2048)
DILATIONS = (1, 4, 16)
N_GROUPS = 4
EXPERTS_PER_GROUP = 8
N_EXPERTS = N_GROUPS * EXPERTS_PER_GROUP
TOP_K = 2
Q_BLOCK = 128


def _cparams(sem):
    return pltpu.CompilerParams(dimension_semantics=sem, vmem_limit_bytes=VMEM_LIMIT_BYTES)


def _mm(a, b, prec, trans_b=False):
    dn = (((1,), (1 if trans_b else 0,)), ((), ()))
    if prec == 'bf16':
        return lax.dot_general(a.astype(BF16), b.astype(BF16), dn, preferred_element_type=F32)
    return lax.dot_general(a.astype(F32), b.astype(F32), dn, precision=HIGHEST,
                           preferred_element_type=F32)


def _round_bf16(a):
    return a.astype(BF16).astype(F32)


def _sigmoid(x):
    return 1.0 / (1.0 + jnp.exp(-x))


def _wspec(w, layer, blk, idx):
    if w.ndim == len(blk):
        return pl.BlockSpec(blk, idx)
    return pl.BlockSpec((None,) + tuple(blk), lambda *a: (layer,) + tuple(idx(*a)))


def _rmsnorm_kernel(x_ref, g_ref, o_ref):
    x = x_ref[...]
    y = x * lax.rsqrt(jnp.mean(x * x, axis=-1, keepdims=True) + EPS)
    o_ref[...] = (y * g_ref[...]).astype(o_ref.dtype)


def rmsnorm(x, g, layer, *, out_dtype, tm):
    m, d = x.shape
    g3 = g.reshape(g.shape[0], 1, d)
    return pl.pallas_call(
        _rmsnorm_kernel,
        grid=(m // tm,),
        in_specs=[pl.BlockSpec((tm, d), lambda i: (i, 0)),
                  pl.BlockSpec((None, 1, d), lambda i: (layer, 0, 0))],
        out_specs=pl.BlockSpec((tm, d), lambda i: (i, 0)),
        out_shape=jax.ShapeDtypeStruct((m, d), out_dtype),
        compiler_params=_cparams(("parallel",)),
        name="rmsnorm",
    )(x, g3)


def _matmul_kernel(*refs, prec, has_res):
    if has_res:
        x_ref, w_ref, r_ref, o_ref = refs
    else:
        x_ref, w_ref, o_ref = refs
    acc = _mm(x_ref[...], w_ref[...], prec)
    if has_res:
        acc = r_ref[...] + acc
    o_ref[...] = acc.astype(o_ref.dtype)


def matmul(x, w, *, prec, out_dtype, tm, tn, layer=None, col0=0, ncols=None, residual=None):
    m, k = x.shape
    ncols = w.shape[-1] if ncols is None else ncols
    assert m % tm == 0 and ncols % tn == 0 and col0 % tn == 0
    c0 = col0 // tn
    in_specs = [pl.BlockSpec((tm, k), lambda i, j: (i, 0)),
                _wspec(w, layer, (k, tn), lambda i, j: (0, j + c0))]
    args = [x, w]
    if residual is not None:
        in_specs.append(pl.BlockSpec((tm, tn), lambda i, j: (i, j)))
        args.append(residual)
    return pl.pallas_call(
        functools.partial(_matmul_kernel, prec=prec, has_res=residual is not None),
        grid=(m // tm, ncols // tn),
        in_specs=in_specs,
        out_specs=pl.BlockSpec((tm, tn), lambda i, j: (i, j)),
        out_shape=jax.ShapeDtypeStruct((m, ncols), out_dtype),
        compiler_params=_cparams(("parallel", "arbitrary")),
        name="matmul",
    )(*args)


def _matmul_shift_kernel(x_ref, wa_ref, wb_ref, o_ref, *, shift, prec):
    tn = wa_ref.shape[1]
    w = jnp.concatenate([wa_ref[...], wb_ref[...]], axis=1)[:, shift:shift + tn]
    o_ref[...] = _mm(x_ref[...], w, prec).astype(o_ref.dtype)


def matmul_unaligned(x, w, *, prec, out_dtype, tm, tn, layer, col0, ncols):
    m, k = x.shape
    shift = col0 % LANES
    base = col0 - shift
    assert shift and base % tn == 0 and ncols % tn == 0 and m % tm == 0
    assert col0 + ncols <= w.shape[-1]
    a0 = base // tn
    per = tn // LANES
    b0 = base // LANES + per
    return pl.pallas_call(
        functools.partial(_matmul_shift_kernel, shift=shift, prec=prec),
        grid=(m // tm, ncols // tn),
        in_specs=[pl.BlockSpec((tm, k), lambda i, j: (i, 0)),
                  pl.BlockSpec((None, k, tn), lambda i, j: (layer, 0, a0 + j)),
                  pl.BlockSpec((None, k, LANES), lambda i, j: (layer, 0, b0 + j * per))],
        out_specs=pl.BlockSpec((tm, tn), lambda i, j: (i, j)),
        out_shape=jax.ShapeDtypeStruct((m, ncols), out_dtype),
        compiler_params=_cparams(("parallel", "arbitrary")),
        name="matmul_unaligned",
    )(x, w, w)


def _bcast_rows(a, idxs):
    return jnp.concatenate(
        [jnp.broadcast_to(a[i:i + 1, :], (SUBLANES, a.shape[1])) for i in idxs], axis=0)


def _gla_kernel(q_ref, k_ref, v_ref, r_ref, glr_ref, wup_ref, bg_ref, gout_ref, s0_ref,
                o_ref, sout_ref, st_ref, *, chunk, prec):
    tb = pl.program_id(2)
    rows_per_step = q_ref.shape[0]
    n_chunks = rows_per_step // chunk
    nb = chunk // SUBLANES

    @pl.when(tb == 0)
    def _():
        st_ref[...] = s0_ref[...].T

    row = lax.broadcasted_iota(jnp.int32, (chunk, GLA_DK), 0)
    rr = lax.broadcasted_iota(jnp.int32, (chunk, chunk), 0)
    cc = lax.broadcasted_iota(jnp.int32, (chunk, chunk), 1)
    tril = (cc <= rr).astype(F32)
    levels = []
    m = chunk // 2
    while m >= SUBLANES:
        levels.append(m)
        m //= 2
    wup = wup_ref[...]
    bg = bg_ref[...]
    gout = gout_ref[...]
    scale = GLA_DK ** -0.5

    def chunk_body(c, carry):
        r0 = pl.multiple_of(c * chunk, chunk)
        rows = pl.ds(r0, chunk)
        q = q_ref[rows, :].astype(F32) * scale
        k = k_ref[rows, :].astype(F32)
        v = v_ref[rows, :].astype(F32)
        z = _mm(glr_ref[rows, :], wup, prec) + bg
        la = (jnp.minimum(z, 0.0) - jnp.log1p(jnp.exp(-jnp.abs(z)))) * (1.0 / GLA_TAU)
        b = _mm(tril, la, 'f32')
        st = st_ref[...]
        o = _mm(q * jnp.exp(b), st, prec, trans_b=True)

        a = jnp.zeros((chunk, chunk), F32)
        for m in levels:
            mb = m // SUBLANES
            sh = m.bit_length() - 1
            q_idx = [max((r // mb) * m - 1, 0) for r in range(nb)]
            k_idx = [(r // mb + 1) * m - 1 for r in range(nb)]
            q_role = ((row >> sh) & 1) == 1
            qt = q * jnp.exp(jnp.where(q_role, b - _bcast_rows(b, q_idx), NEG))
            kt = k * jnp.exp(jnp.where(q_role, NEG, _bcast_rows(b, k_idx) - b))
            a_l = _mm(qt, kt, 'f32', trans_b=True)
            a = a + jnp.where((cc >> sh) == (rr >> sh) - 1, a_l, 0.0)
        tl = row & (SUBLANES - 1)
        for j in range(SUBLANES):
            idx = [SUBLANES * r + j for r in range(nb)]
            e = jnp.exp(jnp.where(tl >= j, b - _bcast_rows(b, idx), NEG))
            w = jnp.sum(q * e * _bcast_rows(k, idx), axis=-1, keepdims=True)
            a = a + jnp.where(cc == (rr & -SUBLANES) + j, w, 0.0)
        o = o + _mm(a, v, prec)

        b_end = b[chunk - 1:chunk, :]
        khat = k * jnp.exp(b_end - b)
        dn = (((0,), (0,)), ((), ()))
        if prec == 'bf16':
            upd = lax.dot_general(v.astype(BF16), khat.astype(BF16), dn, preferred_element_type=F32)
        else:
            upd = lax.dot_general(v, khat, dn, precision=HIGHEST, preferred_element_type=F32)
        st_ref[...] = st * jnp.exp(b_end) + upd

        on = o * lax.rsqrt(jnp.mean(o * o, axis=-1, keepdims=True) + EPS) * gout
        r = r_ref[rows, :].astype(F32)
        o_ref[rows, :] = (on * (r * _sigmoid(r))).astype(o_ref.dtype)
        return carry

    lax.fori_loop(0, n_chunks, chunk_body, 0, unroll=min(2, n_chunks))

    @pl.when(tb == pl.num_programs(2) - 1)
    def _():
        sout_ref[...] = st_ref[...].T


def gla(pg, pglr, wup_pad, b_gate, g_out, s0, layer, *, batch, seq, chunk, rows_per_step,
        row0, prec, out_dtype):
    n = batch * seq
    steps = seq // rows_per_step
    kq = GLA_HEADS * GLA_DK // GLA_DK
    kv = 2 * GLA_HEADS * GLA_DK // GLA_DV
    kr = kv + GLA_HEADS
    blk0 = row0 // rows_per_step
    row_map = lambda b, h, t: blk0 + b * steps + t
    out_map = lambda b, h, t: b * steps + t
    bg3 = b_gate.reshape(b_gate.shape[0], 1, -1)
    go3 = g_out.reshape(g_out.shape[0], 1, -1)
    if s0.ndim == 4:
        s0_spec = pl.BlockSpec((None, None, GLA_DK, GLA_DV), lambda b, h, t: (b, h, 0, 0))
    else:
        s0_spec = pl.BlockSpec((None, None, None, GLA_DK, GLA_DV), lambda b, h, t: (layer, b, h, 0, 0))
    return pl.pallas_call(
        functools.partial(_gla_kernel, chunk=chunk, prec=prec),
        grid=(batch, GLA_HEADS, steps),
        in_specs=[
            pl.BlockSpec((rows_per_step, GLA_DK), lambda b, h, t: (row_map(b, h, t), h)),
            pl.BlockSpec((rows_per_step, GLA_DK), lambda b, h, t: (row_map(b, h, t), kq + h)),
            pl.BlockSpec((rows_per_step, GLA_DV), lambda b, h, t: (row_map(b, h, t), kv + h)),
            pl.BlockSpec((rows_per_step, GLA_DV), lambda b, h, t: (row_map(b, h, t), kr + h)),
            pl.BlockSpec((rows_per_step, LANES), lambda b, h, t: (row_map(b, h, t), 0)),
            pl.BlockSpec((None, LANES, GLA_DK), lambda b, h, t: (layer, 0, h)),
            pl.BlockSpec((None, 1, GLA_DK), lambda b, h, t: (layer, 0, h)),
            pl.BlockSpec((None, 1, GLA_DV), lambda b, h, t: (layer, 0, 0)),
            s0_spec,
        ],
        out_specs=[
            pl.BlockSpec((rows_per_step, GLA_DV), lambda b, h, t: (out_map(b, h, t), h)),
            pl.BlockSpec((None, None, GLA_DK, GLA_DV), lambda b, h, t: (b, h, 0, 0)),
        ],
        out_shape=[jax.ShapeDtypeStruct((n, GLA_HEADS * GLA_DV), out_dtype),
                   jax.ShapeDtypeStruct((batch, GLA_HEADS, GLA_DK, GLA_DV), F32)],
        scratch_shapes=[pltpu.VMEM((GLA_DV, GLA_DK), F32)],
        compiler_params=_cparams(("parallel", "parallel", "arbitrary")),
        name="gla",
    )(pg, pg, pg, pg, pglr, wup_pad, bg3, go3, s0)


def _head_norm(x, g):
    return x * lax.rsqrt(jnp.mean(x * x, axis=-1, keepdims=True) + EPS) * g


def _attn_prompt_kernel(*refs, seq, prec):
    (q0, q1, q2, k0, k1, k2, v0, v1, v2, gq_ref, gk_ref,
     att_ref, kc0, vc0, kc1, vc1, kc2, vc2, qn_s, kn_s, o_s, l_s) = refs
    q_refs, k_refs, v_refs = (q0, q1, q2), (k0, k1, k2), (v0, v1, v2)
    kc_refs, vc_refs = (kc0, kc1, kc2), (vc0, vc1, vc2)
    gq = gq_ref[...]
    gk = gk_ref[...]
    qb_rows = Q_BLOCK
    ri = lax.broadcasted_iota(jnp.int32, (qb_rows, qb_rows), 0)
    ci = lax.broadcasted_iota(jnp.int32, (qb_rows, qb_rows), 1)
    cur_ok = ci <= ri
    prev_ok = ci >= ri

    for g in range(3):
        w = min(WINDOWS[g], seq)
        qn_s[g] = _head_norm(q_refs[g][...].astype(F32), gq) * (HEAD_DIM ** -0.5)
        kn = _head_norm(k_refs[g][...].astype(F32), gk)
        kn_s[g] = kn
        kc_refs[g][...] = kn[seq - w:, :]
        vc_refs[g][...] = v_refs[g][seq - w:, :].astype(F32)

    for g in range(3):
        d = DILATIONS[g]
        assert WINDOWS[g] // d == qb_rows
        sub_len = seq // d
        nqb = sub_len // qb_rows
        qn_g, kn_g, v_g, o_g, l_g = qn_s.at[g], kn_s.at[g], v_refs[g], o_s.at[g], l_s.at[g]

        def block(i, carry, d=d, nqb=nqb, qn_g=qn_g, kn_g=kn_g, v_g=v_g, o_g=o_g, l_g=l_g):
            res = i // nqb
            qb = i % nqb
            cur = pl.ds(res + d * qb_rows * qb, qb_rows, stride=d)
            q = qn_g[cur, :]
            kc = kn_g[cur, :]
            vc = v_g[cur, :].astype(F32)
            s_cur = jnp.where(cur_ok, _mm(q, kc, prec, trans_b=True), NEG)
            m = jnp.max(s_cur, axis=-1, keepdims=True)
            if nqb > 1:
                pb = jnp.maximum(qb - 1, 0)
                prev = pl.ds(res + d * qb_rows * pb, qb_rows, stride=d)
                kp = kn_g[prev, :]
                vp = v_g[prev, :].astype(F32)
                s_prev = jnp.where(prev_ok, _mm(q, kp, prec, trans_b=True), NEG)
                s_prev = s_prev + jnp.where(qb > 0, 0.0, NEG)
                m = jnp.maximum(m, jnp.max(s_prev, axis=-1, keepdims=True))
            p_cur = jnp.exp(s_cur - m)
            den = jnp.sum(p_cur, axis=-1, keepdims=True)
            if nqb > 1:
                p_prev = jnp.exp(s_prev - m)
                den = den + jnp.sum(p_prev, axis=-1, keepdims=True)
            acc = _mm(p_cur / den, vc, prec)
            if nqb > 1:
                acc = acc + _mm(p_prev / den, vp, prec)
            o_g[cur, :] = acc
            l_g[cur, :] = jnp.broadcast_to(m + jnp.log(den), (qb_rows, HEAD_DIM))
            return carry

        lax.fori_loop(0, d * nqb, block, 0, unroll=2)

    def combine(i, carry):
        rows = pl.ds(pl.multiple_of(i * qb_rows, qb_rows), qb_rows)
        l0, l1, l2 = l_s[0, rows, :], l_s[1, rows, :], l_s[2, rows, :]
        m = jnp.maximum(jnp.maximum(l0, l1), l2)
        e0, e1, e2 = jnp.exp(l0 - m), jnp.exp(l1 - m), jnp.exp(l2 - m)
        tot = e0 + e1 + e2
        out = o_s[0, rows, :] * (e0 / tot) + o_s[1, rows, :] * (e1 / tot) + o_s[2, rows, :] * (e2 / tot)
        att_ref[rows, :] = out.astype(att_ref.dtype)
        return carry

    lax.fori_loop(0, seq // qb_rows, combine, 0)


def attn_prompt(pa, g_q, g_k, layer, *, batch, seq, prec, out_dtype):
    hpg = HEADS_PER_GROUP
    nh = 3 * hpg
    gq3 = g_q.reshape(g_q.shape[0], 1, -1)
    gk3 = g_k.reshape(g_k.shape[0], 1, -1)
    in_specs = []
    for part in range(3):
        for g in range(3):
            in_specs.append(pl.BlockSpec(
                (seq, HEAD_DIM), lambda b, j, part=part, g=g: (b, part * nh + g * hpg + j)))
    in_specs += [pl.BlockSpec((None, 1, HEAD_DIM), lambda b, j: (layer, 0, 0))] * 2
    out_specs = [pl.BlockSpec((seq, HEAD_DIM), lambda b, j: (b, j))]
    out_shape = [jax.ShapeDtypeStruct((batch * seq, hpg * HEAD_DIM), out_dtype)]
    for g in range(3):
        w = min(WINDOWS[g], seq)
        for _ in range(2):
            out_specs.append(pl.BlockSpec((None, w, HEAD_DIM), lambda b, j: (b, 0, j)))
            out_shape.append(jax.ShapeDtypeStruct((batch, w, hpg * HEAD_DIM), F32))
    res = pl.pallas_call(
        functools.partial(_attn_prompt_kernel, seq=seq, prec=prec),
        grid=(batch, hpg),
        in_specs=in_specs,
        out_specs=out_specs,
        out_shape=out_shape,
        scratch_shapes=[pltpu.VMEM((3, seq, HEAD_DIM), F32)] * 4,
        compiler_params=_cparams(("parallel", "parallel")),
        name="attn_prompt",
    )(*([pa] * 9), gq3, gk3)
    return res[0], [(res[1 + 2 * g], res[2 + 2 * g]) for g in range(3)]


def _attn_sample_kernel(*refs, n_new, prec):
    (q0, q1, q2, k0, k1, k2, v0, v1, v2, c0, c1, c2, gq_ref, gk_ref,
     att_ref, kn0, vn0, kn1, vn1, kn2, vn2) = refs
    q_refs, k_refs, v_refs = (q0, q1, q2), (k0, k1, k2), (v0, v1, v2)
    c_refs = (c0, c1, c2)
    kn_refs, vn_refs = (kn0, kn1, kn2), (vn0, vn1, vn2)
    hpg = HEADS_PER_GROUP
    head = pl.program_id(1)
    gq = gq_ref[...]
    gk = gk_ref[...]
    rnd = _round_bf16 if prec == 'bf16' else (lambda a: a)
    outs, lses = [], []
    for g in range(3):
        d = DILATIONS[g]
        nk = WINDOWS[g] // d
        clen = c_refs[g].shape[0] // (2 * hpg)
        ck = c_refs[g][pl.ds(head, clen, stride=2 * hpg), :]
        cv = c_refs[g][pl.ds(hpg + head, clen, stride=2 * hpg), :]
        q = _head_norm(q_refs[g][...], gq) * (HEAD_DIM ** -0.5)
        kn = _head_norm(k_refs[g][...], gk)
        vn = v_refs[g][...]
        kn_refs[g][...] = kn
        vn_refs[g][...] = vn
        s_c = _mm(q, ck, prec, trans_b=True)
        t_c = lax.broadcasted_iota(jnp.int32, (n_new, clen), 0)
        n_c = lax.broadcasted_iota(jnp.int32, (n_new, clen), 1)
        off = clen + t_c - n_c
        s_c = jnp.where(((off & (d - 1)) == 0) & (off <= nk * d), s_c, NEG)
        t_n = lax.broadcasted_iota(jnp.int32, (n_new, n_new), 0)
        n_n = lax.broadcasted_iota(jnp.int32, (n_new, n_new), 1)
        s_n = jnp.zeros((n_new, n_new), F32)
        q_r, kn_r, vn_r = rnd(q), rnd(kn), rnd(vn)
        for j in range(n_new):
            col = jnp.sum(q_r * kn_r[j:j + 1, :], axis=-1, keepdims=True)
            s_n = s_n + jnp.where(n_n == j, col, 0.0)
        off_n = t_n - n_n
        s_n = jnp.where((off_n >= 0) & ((off_n & (d - 1)) == 0) & (off_n <= nk * d), s_n, NEG)
        m = jnp.maximum(jnp.max(s_c, axis=-1, keepdims=True), jnp.max(s_n, axis=-1, keepdims=True))
        p_c = jnp.exp(s_c - m)
        p_n = jnp.exp(s_n - m)
        den = jnp.sum(p_c, axis=-1, keepdims=True) + jnp.sum(p_n, axis=-1, keepdims=True)
        acc = _mm(p_c / den, cv, prec)
        pn_r = rnd(p_n / den)
        for j in range(n_new):
            acc = acc + pn_r[:, j:j + 1] * vn_r[j:j + 1, :]
        outs.append(acc)
        lses.append(m + jnp.log(den))
    m = jnp.maximum(jnp.maximum(lses[0], lses[1]), lses[2])
    es = [jnp.exp(l - m) for l in lses]
    tot = es[0] + es[1] + es[2]
    att_ref[...] = (outs[0] * (es[0] / tot) + outs[1] * (es[1] / tot)
                    + outs[2] * (es[2] / tot)).astype(att_ref.dtype)


def attn_sample(pa, caches, g_q, g_k, layer, *, batch, n_new, row0, prec, out_dtype):
    hpg = HEADS_PER_GROUP
    nh = 3 * hpg
    gq3 = g_q.reshape(g_q.shape[0], 1, -1)
    gk3 = g_k.reshape(g_k.shape[0], 1, -1)
    blk0 = row0 // n_new
    in_specs = []
    for part in range(3):
        for g in range(3):
            in_specs.append(pl.BlockSpec(
                (n_new, HEAD_DIM), lambda b, j, part=part, g=g: (blk0 + b, part * nh + g * hpg + j)))
    cache_args = []
    for g in range(3):
        c = caches[g]
        clen = c.shape[2]
        in_specs.append(pl.BlockSpec((None, None, clen * 2 * hpg, HEAD_DIM), lambda b, j: (layer, b, 0, 0)))
        cache_args.append(c.reshape(c.shape[0], c.shape[1], clen * 2 * hpg, HEAD_DIM))
    in_specs += [pl.BlockSpec((None, 1, HEAD_DIM), lambda b, j: (layer, 0, 0))] * 2
    out_specs = [pl.BlockSpec((n_new, HEAD_DIM), lambda b, j: (b, j))]
    out_shape = [jax.ShapeDtypeStruct((batch * n_new, hpg * HEAD_DIM), out_dtype)]
    for g in range(3):
        for _ in range(2):
            out_specs.append(pl.BlockSpec((n_new, HEAD_DIM), lambda b, j: (b, j)))
            out_shape.append(jax.ShapeDtypeStruct((batch * n_new, hpg * HEAD_DIM), F32))
    res = pl.pallas_call(
        functools.partial(_attn_sample_kernel, n_new=n_new, prec=prec),
        grid=(batch, hpg),
        in_specs=in_specs,
        out_specs=out_specs,
        out_shape=out_shape,
        compiler_params=_cparams(("parallel", "parallel")),
        name="attn_sample",
    )(*([pa] * 9), *cache_args, gq3, gk3)
    return res[0], [(res[1 + 2 * g], res[2 + 2 * g]) for g in range(3)]


def _merge_kernel(h_ref, og_ref, at_ref, wg1_ref, wg2_ref, wgp_ref, wap_ref, o_ref, *, prec):
    h = h_ref[...]
    g1 = _mm(h, wg1_ref[...], prec)
    g2 = _mm(h, wg2_ref[...], prec)
    a = _mm(og_ref[...], wgp_ref[...], prec)
    b = _mm(at_ref[...], wap_ref[...], prec)
    o_ref[...] = (_sigmoid(g1) * a + _sigmoid(g2) * b).astype(o_ref.dtype)


def merge(h, og, att, wmg, wgp, wap, layer, *, prec, out_dtype, tm, tn):
    m, d = h.shape
    nj = d // tn
    return pl.pallas_call(
        functools.partial(_merge_kernel, prec=prec),
        grid=(m // tm, nj),
        in_specs=[
            pl.BlockSpec((tm, d), lambda i, j: (i, 0)),
            pl.BlockSpec((tm, og.shape[1]), lambda i, j: (i, 0)),
            pl.BlockSpec((tm, att.shape[1]), lambda i, j: (i, 0)),
            _wspec(wmg, layer, (d, tn), lambda i, j: (0, j)),
            _wspec(wmg, layer, (d, tn), lambda i, j: (0, nj + j)),
            _wspec(wgp, layer, (og.shape[1], tn), lambda i, j: (0, j)),
            _wspec(wap, layer, (att.shape[1], tn), lambda i, j: (0, j)),
        ],
        out_specs=pl.BlockSpec((tm, tn), lambda i, j: (i, j)),
        out_shape=jax.ShapeDtypeStruct((m, d), out_dtype),
        compiler_params=_cparams(("parallel", "arbitrary")),
        name="merge",
    )(h, og, att, wmg, wmg, wgp, wap)


def _router_kernel(x_ref, g_ref, wr_ref, h_ref, route_ref, cnt_ref, carry_ref):
    @pl.when(pl.program_id(0) == 0)
    def _():
        carry_ref[...] = jnp.zeros_like(carry_ref)

    x = x_ref[...]
    h = x * lax.rsqrt(jnp.mean(x * x, axis=-1, keepdims=True) + EPS) * g_ref[...]
    h_ref[...] = h.astype(h_ref.dtype)
    logits = _mm(h, wr_ref[...], 'bf16')
    lane = lax.broadcasted_iota(jnp.int32, logits.shape, 1)
    lane_f = lane.astype(F32)
    big = float(LANES)
    is_g = lane < N_GROUPS
    gl = jnp.where(is_g, logits, NEG)
    gmax = jnp.max(gl, axis=-1, keepdims=True)
    gsel = jnp.min(jnp.where(gl == gmax, lane_f, big), axis=-1, keepdims=True)
    gprob = 1.0 / jnp.sum(jnp.where(is_g, jnp.exp(gl - gmax), 0.0), axis=-1, keepdims=True)
    e_idx = lane - N_GROUPS
    e_grp = e_idx >> (EXPERTS_PER_GROUP.bit_length() - 1)
    in_grp = (e_idx >= 0) & (e_idx < N_EXPERTS) & (e_grp.astype(F32) == gsel)
    el = jnp.where(in_grp, logits, NEG)
    v1 = jnp.max(el, axis=-1, keepdims=True)
    i1 = jnp.min(jnp.where(el == v1, lane_f, big), axis=-1, keepdims=True)
    el2 = jnp.where(lane_f == i1, NEG, el)
    v2 = jnp.max(el2, axis=-1, keepdims=True)
    i2 = jnp.min(jnp.where(el2 == v2, lane_f, big), axis=-1, keepdims=True)
    t = jnp.exp(v2 - v1)
    w1 = gprob / (1.0 + t)
    w2 = gprob * (t / (1.0 + t))
    tm = x.shape[0]
    picked = ((lane_f == i1) | (lane_f == i2)).astype(F32)
    rr = lax.broadcasted_iota(jnp.int32, (tm, tm), 0)
    cc = lax.broadcasted_iota(jnp.int32, (tm, tm), 1)
    before = _mm((cc < rr).astype(F32), picked, 'bf16') + carry_ref[...]
    r1 = jnp.sum(jnp.where(lane_f == i1, before, 0.0), axis=-1, keepdims=True)
    r2 = jnp.sum(jnp.where(lane_f == i2, before, 0.0), axis=-1, keepdims=True)
    carry_ref[...] = carry_ref[...] + jnp.sum(picked, axis=0, keepdims=True)
    cnt_ref[...] = carry_ref[...]
    vals = (i1 - N_GROUPS, i2 - N_GROUPS, w1, w2, r1, r2)
    route = jnp.zeros_like(logits)
    for k, v in enumerate(vals):
        route = jnp.where(lane == k, v, route)
    route_ref[...] = route


def router(x, g, wr, layer, *, tm):
    m, d = x.shape
    g3 = g.reshape(g.shape[0], 1, d)
    return pl.pallas_call(
        _router_kernel,
        grid=(m // tm,),
        in_specs=[pl.BlockSpec((tm, d), lambda i: (i, 0)),
                  pl.BlockSpec((None, 1, d), lambda i: (layer, 0, 0)),
                  pl.BlockSpec((None, d, LANES), lambda i: (layer, 0, 0))],
        out_specs=[pl.BlockSpec((tm, d), lambda i: (i, 0)),
                   pl.BlockSpec((tm, LANES), lambda i: (i, 0)),
                   pl.BlockSpec((1, LANES), lambda i: (0, 0))],
        out_shape=[jax.ShapeDtypeStruct((m, d), F32),
                   jax.ShapeDtypeStruct((m, LANES), F32),
                   jax.ShapeDtypeStruct((1, LANES), F32)],
        scratch_shapes=[pltpu.VMEM((1, LANES), F32)],
        compiler_params=_cparams(("arbitrary",)),
        name="router",
    )(x, g3, wr)


def _row_copy(src_hbm, dst, sem, src_row, dst_row):
    return pltpu.make_async_copy(src_hbm.at[pl.ds(src_row, 1)], dst.at[pl.ds(dst_row, 1)], sem)


def _row_source_kernel(pos_ref, out_ref):
    def clear(r, c):
        out_ref[r] = 0
        return c

    def put(i, c):
        out_ref[pos_ref[i]] = i // TOP_K
        return c

    lax.fori_loop(0, out_ref.shape[0], clear, 0, unroll=8)
    lax.fori_loop(0, pos_ref.shape[0], put, 0, unroll=8)


def row_source(pos, *, rows):
    return pl.pallas_call(
        _row_source_kernel,
        grid_spec=pltpu.PrefetchScalarGridSpec(
            num_scalar_prefetch=1,
            grid=(1,),
            in_specs=[],
            out_specs=pl.BlockSpec(memory_space=pltpu.SMEM),
        ),
        out_shape=jax.ShapeDtypeStruct((rows,), jnp.int32),
        compiler_params=_cparams(("arbitrary",)),
        name="moe_row_source",
    )(pos)


def _tile_row_dmas(src_hbm, dst, sem, n_rows, src_row_of):
    def start():
        def body(i, c):
            _row_copy(src_hbm, dst, sem, src_row_of(i), i).start()
            return c
        lax.fori_loop(0, n_rows, body, 0, unroll=8)

    def wait():
        pltpu.make_async_copy(src_hbm.at[pl.ds(0, n_rows)], dst, sem).wait()

    return start, wait


def _gather_rows_kernel(idx_ref, src_hbm, o_ref, buf, sem):
    rb = o_ref.shape[0]
    step = pl.program_id(0)
    slot = step % 2

    def dmas(tile, s):
        return _tile_row_dmas(src_hbm, buf.at[s], sem.at[s], rb, lambda i: idx_ref[tile * rb + i])

    @pl.when(step == 0)
    def _():
        dmas(0, 0)[0]()

    @pl.when(step + 1 < pl.num_programs(0))
    def _():
        dmas(step + 1, 1 - slot)[0]()

    dmas(step, slot)[1]()
    o_ref[...] = buf[slot]


def gather_rows(src, idx, *, rb):
    rows = idx.shape[0]
    d = src.shape[1]
    return pl.pallas_call(
        _gather_rows_kernel,
        grid_spec=pltpu.PrefetchScalarGridSpec(
            num_scalar_prefetch=1,
            grid=(rows // rb,),
            in_specs=[pl.BlockSpec(memory_space=pl.ANY)],
            out_specs=pl.BlockSpec((rb, d), lambda i, idx: (i, 0)),
            scratch_shapes=[pltpu.VMEM((2, rb, d), src.dtype), pltpu.SemaphoreType.DMA((2,))],
        ),
        out_shape=jax.ShapeDtypeStruct((rows, d), src.dtype),
        compiler_params=_cparams(("arbitrary",)),
        name="moe_gather",
    )(idx, src)


def _moe_up_kernel(blk_e_ref, nused_ref, x_ref, wg_ref, wu_ref, h_ref, *, prec):
    used = pl.program_id(1) < nused_ref[0]

    @pl.when(used)
    def _():
        x = x_ref[...]
        g = _mm(x, wg_ref[...], prec)
        u = _mm(x, wu_ref[...], prec)
        h_ref[...] = (g * _sigmoid(g) * u).astype(h_ref.dtype)

    @pl.when(jnp.logical_not(used))
    def _():
        h_ref[...] = jnp.zeros_like(h_ref)


def _moe_down_kernel(blk_e_ref, nused_ref, h_ref, wd_ref, y_ref, *, prec):
    used = pl.program_id(1) < nused_ref[0]

    @pl.when(used)
    def _():
        y_ref[...] = _mm(h_ref[...], wd_ref[...], prec)

    @pl.when(jnp.logical_not(used))
    def _():
        y_ref[...] = jnp.zeros_like(y_ref)


def moe_experts(xs, blk_e, n_used, w_gate, w_up, w_down, layer, *, rb, tf, tn, prec, h_dtype):
    rows, d = xs.shape
    f = w_gate.shape[-1]
    nb = rows // rb
    hmid = pl.pallas_call(
        functools.partial(_moe_up_kernel, prec=prec),
        grid_spec=pltpu.PrefetchScalarGridSpec(
            num_scalar_prefetch=2,
            grid=(f // tf, nb),
            in_specs=[
                pl.BlockSpec((rb, d), lambda j, i, be, nu: (jnp.minimum(i, nu[0] - 1), 0)),
                pl.BlockSpec((None, None, d, tf), lambda j, i, be, nu: (layer, be[i], 0, j)),
                pl.BlockSpec((None, None, d, tf), lambda j, i, be, nu: (layer, be[i], 0, j)),
            ],
            out_specs=pl.BlockSpec((rb, tf), lambda j, i, be, nu: (i, j)),
        ),
        out_shape=jax.ShapeDtypeStruct((rows, f), h_dtype),
        compiler_params=_cparams(("arbitrary", "arbitrary")),
        name="moe_up",
    )(blk_e, n_used, xs, w_gate, w_up)
    return pl.pallas_call(
        functools.partial(_moe_down_kernel, prec=prec),
        grid_spec=pltpu.PrefetchScalarGridSpec(
            num_scalar_prefetch=2,
            grid=(d // tn, nb),
            in_specs=[
                pl.BlockSpec((rb, f), lambda j, i, be, nu: (i, 0)),
                pl.BlockSpec((None, None, f, tn), lambda j, i, be, nu: (layer, be[i], 0, j)),
            ],
            out_specs=pl.BlockSpec((rb, tn), lambda j, i, be, nu: (i, j)),
        ),
        out_shape=jax.ShapeDtypeStruct((rows, d), F32),
        compiler_params=_cparams(("arbitrary", "arbitrary")),
        name="moe_down",
    )(blk_e, n_used, hmid, w_down)


def _combine_kernel(pos_ref, x_ref, route_ref, y_hbm, o_ref, buf, sem):
    tm = x_ref.shape[0]
    step = pl.program_id(0)
    slot = step % 2

    def dmas(tile, s, k):
        return _tile_row_dmas(y_hbm, buf.at[s, k], sem.at[s], tm,
                              lambda i: pos_ref[TOP_K * (tile * tm + i) + k])

    @pl.when(step == 0)
    def _():
        for k in range(TOP_K):
            dmas(0, 0, k)[0]()

    @pl.when(step + 1 < pl.num_programs(0))
    def _():
        for k in range(TOP_K):
            dmas(step + 1, 1 - slot, k)[0]()

    for k in range(TOP_K):
        dmas(step, slot, k)[1]()
    route = route_ref[...]
    g1 = route[:, TOP_K:TOP_K + 1]
    g2 = route[:, TOP_K + 1:TOP_K + 2]
    o_ref[...] = x_ref[...] + (buf[slot, 0] * g1 + buf[slot, 1] * g2)


def moe_combine(x, y, route, pos, *, tm):
    m, d = x.shape
    return pl.pallas_call(
        _combine_kernel,
        grid_spec=pltpu.PrefetchScalarGridSpec(
            num_scalar_prefetch=1,
            grid=(m // tm,),
            in_specs=[pl.BlockSpec((tm, d), lambda i, pos: (i, 0)),
                      pl.BlockSpec((tm, LANES), lambda i, pos: (i, 0)),
                      pl.BlockSpec(memory_space=pl.ANY)],
            out_specs=pl.BlockSpec((tm, d), lambda i, pos: (i, 0)),
            scratch_shapes=[pltpu.VMEM((2, TOP_K, tm, d), F32), pltpu.SemaphoreType.DMA((2,))],
        ),
        out_shape=jax.ShapeDtypeStruct((m, d), F32),
        compiler_params=_cparams(("arbitrary",)),
        name="moe_combine",
    )(pos, x, route, y)


def _dispatch_plan(route, counts, rb):
    n = route.shape[0]
    experts = jnp.arange(N_EXPERTS, dtype=jnp.int32)
    eid = route[:, :TOP_K].astype(jnp.int32)
    rank = route[:, 2 * TOP_K:3 * TOP_K].astype(jnp.int32)
    counts = counts.astype(jnp.int32)
    padded = (counts + rb - 1) // rb * rb
    pend = jnp.cumsum(padded)
    pstarts = pend - padded
    npair = n * TOP_K
    nb = (npair + N_EXPERTS * (rb - 1) + rb - 1) // rb
    blk_start = jnp.arange(nb, dtype=jnp.int32) * rb
    blk_e = jnp.minimum(jnp.sum(pend[None, :] <= blk_start[:, None], axis=1), N_EXPERTS - 1)
    start_of = jnp.sum(jnp.where(eid[..., None] == experts, pstarts, 0), axis=-1)
    pos = (start_of + rank).reshape(npair).astype(jnp.int32)
    n_used = (pend[-1:] // rb).astype(jnp.int32)
    return pos, blk_e.astype(jnp.int32), n_used, nb * rb


def moe_layer(x, g, wr, w_gate, w_up, w_down, layer, *, prec, rb, tf, tn, tm_router, tm_combine):
    h2, route, cnt = router(x, g, wr, layer, tm=tm_router)
    counts = cnt[0, N_GROUPS:N_GROUPS + N_EXPERTS]
    pos, blk_e, n_used, rows = _dispatch_plan(route, counts, rb)
    xs = gather_rows(h2, row_source(pos, rows=rows), rb=rb)
    h_dtype = BF16 if prec == 'bf16' else F32
    y = moe_experts(xs, blk_e, n_used, w_gate, w_up, w_down, layer,
                    rb=rb, tf=tf, tn=tn, prec=prec, h_dtype=h_dtype)
    return moe_combine(x, y, route, pos, tm=tm_combine)


GLA_COLS = 2 * GLA_HEADS * GLA_DK + 2 * GLA_HEADS * GLA_DV
ATT_COLS = 3 * 3 * HEADS_PER_GROUP * HEAD_DIM


PREC = 'bf16'
MOE_ROW_BLOCK = 128
BF16_ROWS = 16


def _row_tile(n, target):
    best = None
    for t in range(BF16_ROWS, target + 1, BF16_ROWS):
        if n % t == 0:
            best = t
    assert best is not None, (n, target)
    return best


def kernel(x_prompt, x_sample, cache_kv_w128, cache_kv_w512, cache_kv_w2048, state_gla,
           g_norm_mix, w_in, w_gla_gate_up, b_gla_gate, g_q, g_k, g_gla_out,
           w_gla_proj, w_att_proj, w_merge_gate, w_out, g_norm_ffn,
           w_router_group, w_router_expert, w_exp_gate, w_exp_up, w_exp_down):
    caches = (cache_kv_w128, cache_kv_w512, cache_kv_w2048)
    bp, sp, d = x_prompt.shape
    bs, ss, _ = x_sample.shape
    depth = w_in.shape[0]
    hpg = HEADS_PER_GROUP
    n_p, n_s = bp * sp, bs * ss
    n = n_p + n_s
    x = jnp.concatenate([x_prompt.reshape(n_p, d), x_sample.reshape(n_s, d)], axis=0)
    tm_norm = _row_tile(n, 256)
    tm_mm = _row_tile(n, 1536)
    tm_merge = _row_tile(n, 768)
    tm_combine = _row_tile(n, 128)

    glr0 = GLA_COLS
    att0 = GLA_COLS + GLA_RANK
    wup_pad = jnp.pad(w_gla_gate_up, ((0, 0), (0, LANES - GLA_RANK), (0, 0)))
    wr = jnp.pad(jnp.concatenate([w_router_group, w_router_expert], axis=-1),
                 ((0, 0), (0, 0), (0, LANES - N_GROUPS - N_EXPERTS)))
    zero_state = jnp.zeros((bp, GLA_HEADS, GLA_DK, GLA_DV), F32)

    kv_p = [[] for _ in range(3)]
    kv_s = [[] for _ in range(3)]
    st_p, st_s = [], []
    for l in range(depth):
        h = rmsnorm(x, g_norm_mix, l, out_dtype=BF16, tm=tm_norm)
        pg = matmul(h, w_in, prec=PREC, out_dtype=F32, tm=tm_mm, tn=256, layer=l, ncols=GLA_COLS)
        pglr = matmul(h, w_in, prec=PREC, out_dtype=F32, tm=tm_mm, tn=LANES, layer=l, col0=glr0, ncols=LANES)
        pa = matmul_unaligned(h, w_in, prec=PREC, out_dtype=F32, tm=tm_mm, tn=256, layer=l,
                              col0=att0, ncols=ATT_COLS)

        og_p, st = gla(pg, pglr, wup_pad, b_gla_gate, g_gla_out, zero_state, l, batch=bp, seq=sp,
                       chunk=64, rows_per_step=512, row0=0, prec=PREC, out_dtype=BF16)
        st_p.append(st)
        og_s, st = gla(pg, pglr, wup_pad, b_gla_gate, g_gla_out, state_gla, l, batch=bs, seq=ss,
                       chunk=ss, rows_per_step=ss, row0=n_p, prec=PREC, out_dtype=F32)
        st_s.append(st)

        att_p, kvs = attn_prompt(pa, g_q, g_k, l, batch=bp, seq=sp, prec=PREC, out_dtype=BF16)
        for g in range(3):
            w = min(WINDOWS[g], sp)
            kv_p[g].append(jnp.stack([kvs[g][0].reshape(bp, w, hpg, HEAD_DIM),
                                      kvs[g][1].reshape(bp, w, hpg, HEAD_DIM)], axis=2))
        att_s, kvs = attn_sample(pa, caches, g_q, g_k, l, batch=bs, n_new=ss, row0=n_p,
                                 prec=PREC, out_dtype=F32)
        for g in range(3):
            kv_s[g].append(jnp.stack([kvs[g][0].reshape(bs, ss, hpg, HEAD_DIM),
                                      kvs[g][1].reshape(bs, ss, hpg, HEAD_DIM)], axis=2))

        og = jnp.concatenate([og_p, og_s.astype(BF16)], axis=0)
        att = jnp.concatenate([att_p, att_s.astype(BF16)], axis=0)
        mixed = merge(h, og, att, w_merge_gate, w_gla_proj, w_att_proj, l, prec=PREC, out_dtype=BF16,
                      tm=tm_merge, tn=256)
        x = matmul(mixed, w_out, prec=PREC, out_dtype=F32, tm=tm_mm, tn=256, layer=l, residual=x)
        x = moe_layer(x, g_norm_ffn, wr, w_exp_gate, w_exp_up, w_exp_down, l, prec=PREC,
                      rb=MOE_ROW_BLOCK, tf=512, tn=2048, tm_router=tm_norm, tm_combine=tm_combine)

    xp, xs = x[:n_p], x[n_p:]
    return (xp.reshape(bp, sp, d), xs.reshape(bs, ss, d),
            jnp.stack(kv_p[0]), jnp.stack(kv_s[0]),
            jnp.stack(kv_p[1]), jnp.stack(kv_s[1]),
            jnp.stack(kv_p[2]), jnp.stack(kv_s[2]),
            jnp.stack(st_p), jnp.stack(st_s))
```

```python
import functools

import jax
import jax.numpy as jnp
from jax import lax
from jax.experimental import pallas as pl
from jax.experimental.pallas import tpu as pltpu

F32 = jnp.float32
BF16 = jnp.bfloat16
HIGHEST = lax.Precision.HIGHEST

LANES = 128
SUBLANES = 8
VMEM_LIMIT_BYTES = 56 * 1024 * 1024

EPS = 1e-6
NEG = -1e30

GLA_HEADS = 8
GLA_DK = 128
GLA_DV = 256
GLA_TAU = 16.0
GLA_RANK = 16
HEAD_DIM = 128
HEADS_PER_GROUP = 8
WINDOWS = (128, 512, 2048)
DILATIONS = (1, 4, 16)
N_GROUPS = 4
EXPERTS_PER_GROUP = 8
N_EXPERTS = N_GROUPS * EXPERTS_PER_GROUP
TOP_K = 2
Q_BLOCK = 128


def _cparams(sem):
    return pltpu.CompilerParams(dimension_semantics=sem, vmem_limit_bytes=VMEM_LIMIT_BYTES)


def _mm(a, b, prec, trans_b=False):
    dn = (((1,), (1 if trans_b else 0,)), ((), ()))
    if prec == 'bf16':
        return lax.dot_general(a.astype(BF16), b.astype(BF16), dn, preferred_element_type=F32)
    return lax.dot_general(a.astype(F32), b.astype(F32), dn, precision=HIGHEST,
                           preferred_element_type=F32)


def _round_bf16(a):
    return a.astype(BF16).astype(F32)


def _sigmoid(x):
    return 1.0 / (1.0 + jnp.exp(-x))


def _wspec(w, layer, blk, idx):
    if w.ndim == len(blk):
        return pl.BlockSpec(blk, idx)
    return pl.BlockSpec((None,) + tuple(blk), lambda *a: (layer,) + tuple(idx(*a)))


def _rmsnorm_kernel(x_ref, g_ref, o_ref):
    x = x_ref[...]
    y = x * lax.rsqrt(jnp.mean(x * x, axis=-1, keepdims=True) + EPS)
    o_ref[...] = (y * g_ref[...]).astype(o_ref.dtype)


def rmsnorm(x, g, layer, *, out_dtype, tm):
    m, d = x.shape
    g3 = g.reshape(g.shape[0], 1, d)
    return pl.pallas_call(
        _rmsnorm_kernel,
        grid=(m // tm,),
        in_specs=[pl.BlockSpec((tm, d), lambda i: (i, 0)),
                  pl.BlockSpec((None, 1, d), lambda i: (layer, 0, 0))],
        out_specs=pl.BlockSpec((tm, d), lambda i: (i, 0)),
        out_shape=jax.ShapeDtypeStruct((m, d), out_dtype),
        compiler_params=_cparams(("parallel",)),
        name="rmsnorm",
    )(x, g3)


def _matmul_kernel(*refs, prec, has_res, w_is_transposed):
    if has_res:
        x_ref, w_ref, r_ref, o_ref = refs
    else:
        x_ref, w_ref, o_ref = refs
    w = w_ref[0] if w_is_transposed else w_ref[...]
    acc = _mm(x_ref[...], w, prec, trans_b=w_is_transposed)
    if has_res:
        acc = r_ref[...] + acc
    o_ref[...] = acc.astype(o_ref.dtype)


def matmul(x, w, *, prec, out_dtype, tm, tn, layer=None, col0=0, ncols=None, residual=None,
           w_is_transposed=False):
    m, k = x.shape
    n_total = w.shape[-2] if w_is_transposed else w.shape[-1]
    ncols = n_total if ncols is None else ncols
    assert m % tm == 0 and ncols % tn == 0 and col0 + ncols <= n_total
    if w_is_transposed:
        assert col0 % BF16_ROWS == 0 and w.ndim == 3
        w_spec = pl.BlockSpec((pl.Element(1), pl.Element(tn), pl.Element(k)),
                              lambda i, j: (layer, pl.multiple_of(col0 + j * tn, BF16_ROWS), 0))
    else:
        assert col0 % tn == 0
        w_spec = _wspec(w, layer, (k, tn), lambda i, j: (0, j + col0 // tn))
    in_specs = [pl.BlockSpec((tm, k), lambda i, j: (i, 0)), w_spec]
    args = [x, w]
    if residual is not None:
        in_specs.append(pl.BlockSpec((tm, tn), lambda i, j: (i, j)))
        args.append(residual)
    return pl.pallas_call(
        functools.partial(_matmul_kernel, prec=prec, has_res=residual is not None,
                          w_is_transposed=w_is_transposed),
        grid=(m // tm, ncols // tn),
        in_specs=in_specs,
        out_specs=pl.BlockSpec((tm, tn), lambda i, j: (i, j)),
        out_shape=jax.ShapeDtypeStruct((m, ncols), out_dtype),
        compiler_params=_cparams(("parallel", "arbitrary")),
        name="matmul",
    )(*args)


def _bcast_rows(a, idxs):
    return jnp.concatenate(
        [jnp.broadcast_to(a[i:i + 1, :], (SUBLANES, a.shape[1])) for i in idxs], axis=0)


def _gla_kernel(q_ref, k_ref, v_ref, r_ref, glr_ref, wup_ref, bg_ref, gout_ref, s0_ref,
                o_ref, sout_ref, st_ref, *, chunk, prec):
    tb = pl.program_id(2)
    rows_per_step = q_ref.shape[0]
    n_chunks = rows_per_step // chunk
    nb = chunk // SUBLANES

    @pl.when(tb == 0)
    def _():
        st_ref[...] = s0_ref[...].T

    row = lax.broadcasted_iota(jnp.int32, (chunk, GLA_DK), 0)
    rr = lax.broadcasted_iota(jnp.int32, (chunk, chunk), 0)
    cc = lax.broadcasted_iota(jnp.int32, (chunk, chunk), 1)
    tril = (cc <= rr).astype(F32)
    levels = []
    m = chunk // 2
    while m >= SUBLANES:
        levels.append(m)
        m //= 2
    wup = wup_ref[...]
    bg = bg_ref[...]
    gout = gout_ref[...]
    scale = GLA_DK ** -0.5

    def chunk_body(c, carry):
        r0 = pl.multiple_of(c * chunk, chunk)
        rows = pl.ds(r0, chunk)
        q = q_ref[rows, :].astype(F32) * scale
        k = k_ref[rows, :].astype(F32)
        v = v_ref[rows, :].astype(F32)
        z = _mm(glr_ref[rows, :], wup, prec) + bg
        la = (jnp.minimum(z, 0.0) - jnp.log1p(jnp.exp(-jnp.abs(z)))) * (1.0 / GLA_TAU)
        b = _mm(tril, la, 'f32')
        st = st_ref[...]
        o = _mm(q * jnp.exp(b), st, prec, trans_b=True)

        a = jnp.zeros((chunk, chunk), F32)
        for m in levels:
            mb = m // SUBLANES
            sh = m.bit_length() - 1
            q_idx = [max((r // mb) * m - 1, 0) for r in range(nb)]
            k_idx = [(r // mb + 1) * m - 1 for r in range(nb)]
            q_role = ((row >> sh) & 1) == 1
            qt = q * jnp.exp(jnp.where(q_role, b - _bcast_rows(b, q_idx), NEG))
            kt = k * jnp.exp(jnp.where(q_role, NEG, _bcast_rows(b, k_idx) - b))
            a_l = _mm(qt, kt, 'f32', trans_b=True)
            a = a + jnp.where((cc >> sh) == (rr >> sh) - 1, a_l, 0.0)
        tl = row & (SUBLANES - 1)
        for j in range(SUBLANES):
            idx = [SUBLANES * r + j for r in range(nb)]
            e = jnp.exp(jnp.where(tl >= j, b - _bcast_rows(b, idx), NEG))
            w = jnp.sum(q * e * _bcast_rows(k, idx), axis=-1, keepdims=True)
            a = a + jnp.where(cc == (rr & -SUBLANES) + j, w, 0.0)
        o = o + _mm(a, v, prec)

        b_end = b[chunk - 1:chunk, :]
        khat = k * jnp.exp(b_end - b)
        dn = (((0,), (0,)), ((), ()))
        if prec == 'bf16':
            upd = lax.dot_general(v.astype(BF16), khat.astype(BF16), dn, preferred_element_type=F32)
        else:
            upd = lax.dot_general(v, khat, dn, precision=HIGHEST, preferred_element_type=F32)
        st_ref[...] = st * jnp.exp(b_end) + upd

        on = o * lax.rsqrt(jnp.mean(o * o, axis=-1, keepdims=True) + EPS) * gout
        r = r_ref[rows, :].astype(F32)
        o_ref[rows, :] = (on * (r * _sigmoid(r))).astype(o_ref.dtype)
        return carry

    lax.fori_loop(0, n_chunks, chunk_body, 0, unroll=min(2, n_chunks))

    @pl.when(tb == pl.num_programs(2) - 1)
    def _():
        sout_ref[...] = st_ref[...].T


def gla(pg, pglr, wup_pad, b_gate, g_out, s0, layer, *, batch, seq, chunk, rows_per_step,
        row0, prec, out_dtype):
    n = batch * seq
    steps = seq // rows_per_step
    kq = GLA_HEADS * GLA_DK // GLA_DK
    kv = 2 * GLA_HEADS * GLA_DK // GLA_DV
    kr = kv + GLA_HEADS
    blk0 = row0 // rows_per_step
    row_map = lambda b, h, t: blk0 + b * steps + t
    out_map = lambda b, h, t: b * steps + t
    bg3 = b_gate.reshape(b_gate.shape[0], 1, -1)
    go3 = g_out.reshape(g_out.shape[0], 1, -1)
    if s0.ndim == 4:
        s0_spec = pl.BlockSpec((None, None, GLA_DK, GLA_DV), lambda b, h, t: (b, h, 0, 0))
    else:
        s0_spec = pl.BlockSpec((None, None, None, GLA_DK, GLA_DV), lambda b, h, t: (layer, b, h, 0, 0))
    return pl.pallas_call(
        functools.partial(_gla_kernel, chunk=chunk, prec=prec),
        grid=(batch, GLA_HEADS, steps),
        in_specs=[
            pl.BlockSpec((rows_per_step, GLA_DK), lambda b, h, t: (row_map(b, h, t), h)),
            pl.BlockSpec((rows_per_step, GLA_DK), lambda b, h, t: (row_map(b, h, t), kq + h)),
            pl.BlockSpec((rows_per_step, GLA_DV), lambda b, h, t: (row_map(b, h, t), kv + h)),
            pl.BlockSpec((rows_per_step, GLA_DV), lambda b, h, t: (row_map(b, h, t), kr + h)),
            pl.BlockSpec((rows_per_step, LANES), lambda b, h, t: (row_map(b, h, t), 0)),
            pl.BlockSpec((None, LANES, GLA_DK), lambda b, h, t: (layer, 0, h)),
            pl.BlockSpec((None, 1, GLA_DK), lambda b, h, t: (layer, 0, h)),
            pl.BlockSpec((None, 1, GLA_DV), lambda b, h, t: (layer, 0, 0)),
            s0_spec,
        ],
        out_specs=[
            pl.BlockSpec((rows_per_step, GLA_DV), lambda b, h, t: (out_map(b, h, t), h)),
            pl.BlockSpec((None, None, GLA_DK, GLA_DV), lambda b, h, t: (b, h, 0, 0)),
        ],
        out_shape=[jax.ShapeDtypeStruct((n, GLA_HEADS * GLA_DV), out_dtype),
                   jax.ShapeDtypeStruct((batch, GLA_HEADS, GLA_DK, GLA_DV), F32)],
        scratch_shapes=[pltpu.VMEM((GLA_DV, GLA_DK), F32)],
        compiler_params=_cparams(("parallel", "parallel", "arbitrary")),
        name="gla",
    )(pg, pg, pg, pg, pglr, wup_pad, bg3, go3, s0)


def _head_norm(x, g):
    return x * lax.rsqrt(jnp.mean(x * x, axis=-1, keepdims=True) + EPS) * g


def _attn_prompt_kernel(*refs, seq, prec):
    (q0, q1, q2, k0, k1, k2, v0, v1, v2, gq_ref, gk_ref,
     att_ref, kc0, vc0, kc1, vc1, kc2, vc2, qn_s, kn_s, o_s, l_s) = refs
    q_refs, k_refs, v_refs = (q0, q1, q2), (k0, k1, k2), (v0, v1, v2)
    kc_refs, vc_refs = (kc0, kc1, kc2), (vc0, vc1, vc2)
    gq = gq_ref[...]
    gk = gk_ref[...]
    qb_rows = Q_BLOCK
    ri = lax.broadcasted_iota(jnp.int32, (qb_rows, qb_rows), 0)
    ci = lax.broadcasted_iota(jnp.int32, (qb_rows, qb_rows), 1)
    cur_ok = ci <= ri
    prev_ok = ci >= ri

    for g in range(3):
        w = min(WINDOWS[g], seq)
        qn_s[g] = _head_norm(q_refs[g][...].astype(F32), gq) * (HEAD_DIM ** -0.5)
        kn = _head_norm(k_refs[g][...].astype(F32), gk)
        kn_s[g] = kn
        kc_refs[g][...] = kn[seq - w:, :]
        vc_refs[g][...] = v_refs[g][seq - w:, :].astype(F32)

    for g in range(3):
        d = DILATIONS[g]
        assert WINDOWS[g] // d == qb_rows
        sub_len = seq // d
        nqb = sub_len // qb_rows
        qn_g, kn_g, v_g, o_g, l_g = qn_s.at[g], kn_s.at[g], v_refs[g], o_s.at[g], l_s.at[g]

        def block(i, carry, d=d, nqb=nqb, qn_g=qn_g, kn_g=kn_g, v_g=v_g, o_g=o_g, l_g=l_g):
            res = i // nqb
            qb = i % nqb
            cur = pl.ds(res + d * qb_rows * qb, qb_rows, stride=d)
            q = qn_g[cur, :]
            kc = kn_g[cur, :]
            vc = v_g[cur, :].astype(F32)
            s_cur = jnp.where(cur_ok, _mm(q, kc, prec, trans_b=True), NEG)
            m = jnp.max(s_cur, axis=-1, keepdims=True)
            if nqb > 1:
                pb = jnp.maximum(qb - 1, 0)
                prev = pl.ds(res + d * qb_rows * pb, qb_rows, stride=d)
                kp = kn_g[prev, :]
                vp = v_g[prev, :].astype(F32)
                s_prev = jnp.where(prev_ok, _mm(q, kp, prec, trans_b=True), NEG)
                s_prev = s_prev + jnp.where(qb > 0, 0.0, NEG)
                m = jnp.maximum(m, jnp.max(s_prev, axis=-1, keepdims=True))
            p_cur = jnp.exp(s_cur - m)
            den = jnp.sum(p_cur, axis=-1, keepdims=True)
            if nqb > 1:
                p_prev = jnp.exp(s_prev - m)
                den = den + jnp.sum(p_prev, axis=-1, keepdims=True)
            acc = _mm(p_cur / den, vc, prec)
            if nqb > 1:
                acc = acc + _mm(p_prev / den, vp, prec)
            o_g[cur, :] = acc
            l_g[cur, :] = jnp.broadcast_to(m + jnp.log(den), (qb_rows, HEAD_DIM))
            return carry

        lax.fori_loop(0, d * nqb, block, 0, unroll=2)

    def combine(i, carry):
        rows = pl.ds(pl.multiple_of(i * qb_rows, qb_rows), qb_rows)
        l0, l1, l2 = l_s[0, rows, :], l_s[1, rows, :], l_s[2, rows, :]
        m = jnp.maximum(jnp.maximum(l0, l1), l2)
        e0, e1, e2 = jnp.exp(l0 - m), jnp.exp(l1 - m), jnp.exp(l2 - m)
        tot = e0 + e1 + e2
        out = o_s[0, rows, :] * (e0 / tot) + o_s[1, rows, :] * (e1 / tot) + o_s[2, rows, :] * (e2 / tot)
        att_ref[rows, :] = out.astype(att_ref.dtype)
        return carry

    lax.fori_loop(0, seq // qb_rows, combine, 0)


def attn_prompt(pa, g_q, g_k, layer, *, batch, seq, prec, out_dtype):
    hpg = HEADS_PER_GROUP
    nh = 3 * hpg
    gq3 = g_q.reshape(g_q.shape[0], 1, -1)
    gk3 = g_k.reshape(g_k.shape[0], 1, -1)
    in_specs = []
    for part in range(3):
        for g in range(3):
            in_specs.append(pl.BlockSpec(
                (seq, HEAD_DIM), lambda b, j, part=part, g=g: (b, part * nh + g * hpg + j)))
    in_specs += [pl.BlockSpec((None, 1, HEAD_DIM), lambda b, j: (layer, 0, 0))] * 2
    out_specs = [pl.BlockSpec((seq, HEAD_DIM), lambda b, j: (b, j))]
    out_shape = [jax.ShapeDtypeStruct((batch * seq, hpg * HEAD_DIM), out_dtype)]
    for g in range(3):
        w = min(WINDOWS[g], seq)
        for _ in range(2):
            out_specs.append(pl.BlockSpec((None, w, HEAD_DIM), lambda b, j: (b, 0, j)))
            out_shape.append(jax.ShapeDtypeStruct((batch, w, hpg * HEAD_DIM), F32))
    res = pl.pallas_call(
        functools.partial(_attn_prompt_kernel, seq=seq, prec=prec),
        grid=(batch, hpg),
        in_specs=in_specs,
        out_specs=out_specs,
        out_shape=out_shape,
        scratch_shapes=[pltpu.VMEM((3, seq, HEAD_DIM), F32)] * 4,
        compiler_params=_cparams(("parallel", "parallel")),
        name="attn_prompt",
    )(*([pa] * 9), gq3, gk3)
    return res[0], [(res[1 + 2 * g], res[2 + 2 * g]) for g in range(3)]


def _attn_sample_kernel(*refs, n_new, prec):
    (q0, q1, q2, k0, k1, k2, v0, v1, v2, c0, c1, c2, gq_ref, gk_ref,
     att_ref, kn0, vn0, kn1, vn1, kn2, vn2) = refs
    q_refs, k_refs, v_refs = (q0, q1, q2), (k0, k1, k2), (v0, v1, v2)
    c_refs = (c0, c1, c2)
    kn_refs, vn_refs = (kn0, kn1, kn2), (vn0, vn1, vn2)
    hpg = HEADS_PER_GROUP
    head = pl.program_id(1)
    gq = gq_ref[...]
    gk = gk_ref[...]
    rnd = _round_bf16 if prec == 'bf16' else (lambda a: a)
    outs, lses = [], []
    for g in range(3):
        d = DILATIONS[g]
        nk = WINDOWS[g] // d
        clen = c_refs[g].shape[0] // (2 * hpg)
        ck = c_refs[g][pl.ds(head, clen, stride=2 * hpg), :]
        cv = c_refs[g][pl.ds(hpg + head, clen, stride=2 * hpg), :]
        q = _head_norm(q_refs[g][...], gq) * (HEAD_DIM ** -0.5)
        kn = _head_norm(k_refs[g][...], gk)
        vn = v_refs[g][...]
        kn_refs[g][...] = kn
        vn_refs[g][...] = vn
        s_c = _mm(q, ck, prec, trans_b=True)
        t_c = lax.broadcasted_iota(jnp.int32, (n_new, clen), 0)
        n_c = lax.broadcasted_iota(jnp.int32, (n_new, clen), 1)
        off = clen + t_c - n_c
        s_c = jnp.where(((off & (d - 1)) == 0) & (off <= nk * d), s_c, NEG)
        t_n = lax.broadcasted_iota(jnp.int32, (n_new, n_new), 0)
        n_n = lax.broadcasted_iota(jnp.int32, (n_new, n_new), 1)
        s_n = jnp.zeros((n_new, n_new), F32)
        q_r, kn_r, vn_r = rnd(q), rnd(kn), rnd(vn)
        for j in range(n_new):
            col = jnp.sum(q_r * kn_r[j:j + 1, :], axis=-1, keepdims=True)
            s_n = s_n + jnp.where(n_n == j, col, 0.0)
        off_n = t_n - n_n
        s_n = jnp.where((off_n >= 0) & ((off_n & (d - 1)) == 0) & (off_n <= nk * d), s_n, NEG)
        m = jnp.maximum(jnp.max(s_c, axis=-1, keepdims=True), jnp.max(s_n, axis=-1, keepdims=True))
        p_c = jnp.exp(s_c - m)
        p_n = jnp.exp(s_n - m)
        den = jnp.sum(p_c, axis=-1, keepdims=True) + jnp.sum(p_n, axis=-1, keepdims=True)
        acc = _mm(p_c / den, cv, prec)
        pn_r = rnd(p_n / den)
        for j in range(n_new):
            acc = acc + pn_r[:, j:j + 1] * vn_r[j:j + 1, :]
        outs.append(acc)
        lses.append(m + jnp.log(den))
    m = jnp.maximum(jnp.maximum(lses[0], lses[1]), lses[2])
    es = [jnp.exp(l - m) for l in lses]
    tot = es[0] + es[1] + es[2]
    att_ref[...] = (outs[0] * (es[0] / tot) + outs[1] * (es[1] / tot)
                    + outs[2] * (es[2] / tot)).astype(att_ref.dtype)


def attn_sample(pa, caches, g_q, g_k, layer, *, batch, n_new, row0, prec, out_dtype):
    hpg = HEADS_PER_GROUP
    nh = 3 * hpg
    gq3 = g_q.reshape(g_q.shape[0], 1, -1)
    gk3 = g_k.reshape(g_k.shape[0], 1, -1)
    blk0 = row0 // n_new
    in_specs = []
    for part in range(3):
        for g in range(3):
            in_specs.append(pl.BlockSpec(
                (n_new, HEAD_DIM), lambda b, j, part=part, g=g: (blk0 + b, part * nh + g * hpg + j)))
    cache_args = []
    for g in range(3):
        c = caches[g]
        clen = c.shape[2]
        in_specs.append(pl.BlockSpec((None, None, clen * 2 * hpg, HEAD_DIM), lambda b, j: (layer, b, 0, 0)))
        cache_args.append(c.reshape(c.shape[0], c.shape[1], clen * 2 * hpg, HEAD_DIM))
    in_specs += [pl.BlockSpec((None, 1, HEAD_DIM), lambda b, j: (layer, 0, 0))] * 2
    out_specs = [pl.BlockSpec((n_new, HEAD_DIM), lambda b, j: (b, j))]
    out_shape = [jax.ShapeDtypeStruct((batch * n_new, hpg * HEAD_DIM), out_dtype)]
    for g in range(3):
        for _ in range(2):
            out_specs.append(pl.BlockSpec((n_new, HEAD_DIM), lambda b, j: (b, j)))
            out_shape.append(jax.ShapeDtypeStruct((batch * n_new, hpg * HEAD_DIM), F32))
    res = pl.pallas_call(
        functools.partial(_attn_sample_kernel, n_new=n_new, prec=prec),
        grid=(batch, hpg),
        in_specs=in_specs,
        out_specs=out_specs,
        out_shape=out_shape,
        compiler_params=_cparams(("parallel", "parallel")),
        name="attn_sample",
    )(*([pa] * 9), *cache_args, gq3, gk3)
    return res[0], [(res[1 + 2 * g], res[2 + 2 * g]) for g in range(3)]


def _merge_kernel(h_ref, og_ref, at_ref, wg1_ref, wg2_ref, wgp_ref, wap_ref, o_ref, *, prec):
    h = h_ref[...]
    g1 = _mm(h, wg1_ref[...], prec)
    g2 = _mm(h, wg2_ref[...], prec)
    a = _mm(og_ref[...], wgp_ref[...], prec)
    b = _mm(at_ref[...], wap_ref[...], prec)
    o_ref[...] = (_sigmoid(g1) * a + _sigmoid(g2) * b).astype(o_ref.dtype)


def merge(h, og, att, wmg, wgp, wap, layer, *, prec, out_dtype, tm, tn):
    m, d = h.shape
    nj = d // tn
    return pl.pallas_call(
        functools.partial(_merge_kernel, prec=prec),
        grid=(m // tm, nj),
        in_specs=[
            pl.BlockSpec((tm, d), lambda i, j: (i, 0)),
            pl.BlockSpec((tm, og.shape[1]), lambda i, j: (i, 0)),
            pl.BlockSpec((tm, att.shape[1]), lambda i, j: (i, 0)),
            _wspec(wmg, layer, (d, tn), lambda i, j: (0, j)),
            _wspec(wmg, layer, (d, tn), lambda i, j: (0, nj + j)),
            _wspec(wgp, layer, (og.shape[1], tn), lambda i, j: (0, j)),
            _wspec(wap, layer, (att.shape[1], tn), lambda i, j: (0, j)),
        ],
        out_specs=pl.BlockSpec((tm, tn), lambda i, j: (i, j)),
        out_shape=jax.ShapeDtypeStruct((m, d), out_dtype),
        compiler_params=_cparams(("parallel", "arbitrary")),
        name="merge",
    )(h, og, att, wmg, wmg, wgp, wap)


def _router_kernel(x_ref, g_ref, wr_ref, h_ref, route_ref, cnt_ref, carry_ref):
    @pl.when(pl.program_id(0) == 0)
    def _():
        carry_ref[...] = jnp.zeros_like(carry_ref)

    x = x_ref[...]
    h = x * lax.rsqrt(jnp.mean(x * x, axis=-1, keepdims=True) + EPS) * g_ref[...]
    h_ref[...] = h.astype(h_ref.dtype)
    logits = _mm(h, wr_ref[...], 'bf16')
    lane = lax.broadcasted_iota(jnp.int32, logits.shape, 1)
    lane_f = lane.astype(F32)
    big = float(LANES)
    is_g = lane < N_GROUPS
    gl = jnp.where(is_g, logits, NEG)
    gmax = jnp.max(gl, axis=-1, keepdims=True)
    gsel = jnp.min(jnp.where(gl == gmax, lane_f, big), axis=-1, keepdims=True)
    gprob = 1.0 / jnp.sum(jnp.where(is_g, jnp.exp(gl - gmax), 0.0), axis=-1, keepdims=True)
    e_idx = lane - N_GROUPS
    e_grp = e_idx >> (EXPERTS_PER_GROUP.bit_length() - 1)
    in_grp = (e_idx >= 0) & (e_idx < N_EXPERTS) & (e_grp.astype(F32) == gsel)
    el = jnp.where(in_grp, logits, NEG)
    v1 = jnp.max(el, axis=-1, keepdims=True)
    i1 = jnp.min(jnp.where(el == v1, lane_f, big), axis=-1, keepdims=True)
    el2 = jnp.where(lane_f == i1, NEG, el)
    v2 = jnp.max(el2, axis=-1, keepdims=True)
    i2 = jnp.min(jnp.where(el2 == v2, lane_f, big), axis=-1, keepdims=True)
    t = jnp.exp(v2 - v1)
    w1 = gprob / (1.0 + t)
    w2 = gprob * (t / (1.0 + t))
    tm = x.shape[0]
    picked = ((lane_f == i1) | (lane_f == i2)).astype(F32)
    rr = lax.broadcasted_iota(jnp.int32, (tm, tm), 0)
    cc = lax.broadcasted_iota(jnp.int32, (tm, tm), 1)
    before = _mm((cc < rr).astype(F32), picked, 'bf16') + carry_ref[...]
    r1 = jnp.sum(jnp.where(lane_f == i1, before, 0.0), axis=-1, keepdims=True)
    r2 = jnp.sum(jnp.where(lane_f == i2, before, 0.0), axis=-1, keepdims=True)
    carry_ref[...] = carry_ref[...] + jnp.sum(picked, axis=0, keepdims=True)
    cnt_ref[...] = carry_ref[...]
    vals = (i1 - N_GROUPS, i2 - N_GROUPS, w1, w2, r1, r2)
    route = jnp.zeros_like(logits)
    for k, v in enumerate(vals):
        route = jnp.where(lane == k, v, route)
    route_ref[...] = route


def router(x, g, wr, layer, *, tm):
    m, d = x.shape
    g3 = g.reshape(g.shape[0], 1, d)
    return pl.pallas_call(
        _router_kernel,
        grid=(m // tm,),
        in_specs=[pl.BlockSpec((tm, d), lambda i: (i, 0)),
                  pl.BlockSpec((None, 1, d), lambda i: (layer, 0, 0)),
                  pl.BlockSpec((None, d, LANES), lambda i: (layer, 0, 0))],
        out_specs=[pl.BlockSpec((tm, d), lambda i: (i, 0)),
                   pl.BlockSpec((tm, LANES), lambda i: (i, 0)),
                   pl.BlockSpec((1, LANES), lambda i: (0, 0))],
        out_shape=[jax.ShapeDtypeStruct((m, d), F32),
                   jax.ShapeDtypeStruct((m, LANES), F32),
                   jax.ShapeDtypeStruct((1, LANES), F32)],
        scratch_shapes=[pltpu.VMEM((1, LANES), F32)],
        compiler_params=_cparams(("arbitrary",)),
        name="router",
    )(x, g3, wr)


def _row_copy(src_hbm, dst, sem, src_row, dst_row):
    return pltpu.make_async_copy(src_hbm.at[pl.ds(src_row, 1)], dst.at[pl.ds(dst_row, 1)], sem)


def _row_source_kernel(pos_ref, out_ref):
    def clear(r, c):
        out_ref[r] = 0
        return c

    def put(i, c):
        out_ref[pos_ref[i]] = i // TOP_K
        return c

    lax.fori_loop(0, out_ref.shape[0], clear, 0, unroll=8)
    lax.fori_loop(0, pos_ref.shape[0], put, 0, unroll=8)


def row_source(pos, *, rows):
    return pl.pallas_call(
        _row_source_kernel,
        grid_spec=pltpu.PrefetchScalarGridSpec(
            num_scalar_prefetch=1,
            grid=(1,),
            in_specs=[],
            out_specs=pl.BlockSpec(memory_space=pltpu.SMEM),
        ),
        out_shape=jax.ShapeDtypeStruct((rows,), jnp.int32),
        compiler_params=_cparams(("arbitrary",)),
        name="moe_row_source",
    )(pos)


def _tile_row_dmas(src_hbm, dst, sem, n_rows, src_row_of):
    def run(method):
        def body(i, c):
            getattr(_row_copy(src_hbm, dst, sem, src_row_of(i), i), method)()
            return c
        lax.fori_loop(0, n_rows, body, 0, unroll=8)
    return (lambda: run("start")), (lambda: run("wait"))


def _gather_rows_kernel(idx_ref, src_hbm, o_ref, buf, sem):
    rb = o_ref.shape[0]
    step = pl.program_id(0)
    slot = step % 2

    def dmas(tile, s):
        return _tile_row_dmas(src_hbm, buf.at[s], sem.at[s], rb, lambda i: idx_ref[tile * rb + i])

    @pl.when(step == 0)
    def _():
        dmas(0, 0)[0]()

    @pl.when(step + 1 < pl.num_programs(0))
    def _():
        dmas(step + 1, 1 - slot)[0]()

    dmas(step, slot)[1]()
    o_ref[...] = buf[slot]


def gather_rows(src, idx, *, rb):
    rows = idx.shape[0]
    d = src.shape[1]
    return pl.pallas_call(
        _gather_rows_kernel,
        grid_spec=pltpu.PrefetchScalarGridSpec(
            num_scalar_prefetch=1,
            grid=(rows // rb,),
            in_specs=[pl.BlockSpec(memory_space=pl.ANY)],
            out_specs=pl.BlockSpec((rb, d), lambda i, idx: (i, 0)),
            scratch_shapes=[pltpu.VMEM((2, rb, d), src.dtype), pltpu.SemaphoreType.DMA((2,))],
        ),
        out_shape=jax.ShapeDtypeStruct((rows, d), src.dtype),
        compiler_params=_cparams(("arbitrary",)),
        name="moe_gather",
    )(idx, src)


def _moe_up_kernel(blk_e_ref, nused_ref, x_ref, wg_ref, wu_ref, h_ref, *, prec):
    used = pl.program_id(1) < nused_ref[0]

    @pl.when(used)
    def _():
        x = x_ref[...]
        g = _mm(x, wg_ref[...], prec)
        u = _mm(x, wu_ref[...], prec)
        h_ref[...] = (g * _sigmoid(g) * u).astype(h_ref.dtype)

    @pl.when(jnp.logical_not(used))
    def _():
        h_ref[...] = jnp.zeros_like(h_ref)


def _moe_down_kernel(blk_e_ref, nused_ref, h_ref, wd_ref, y_ref, *, prec):
    used = pl.program_id(1) < nused_ref[0]

    @pl.when(used)
    def _():
        y_ref[...] = _mm(h_ref[...], wd_ref[...], prec)

    @pl.when(jnp.logical_not(used))
    def _():
        y_ref[...] = jnp.zeros_like(y_ref)


def moe_experts(xs, blk_e, n_used, w_gate, w_up, w_down, layer, *, rb, tf, tn, prec, h_dtype):
    rows, d = xs.shape
    f = w_gate.shape[-1]
    nb = rows // rb
    hmid = pl.pallas_call(
        functools.partial(_moe_up_kernel, prec=prec),
        grid_spec=pltpu.PrefetchScalarGridSpec(
            num_scalar_prefetch=2,
            grid=(f // tf, nb),
            in_specs=[
                pl.BlockSpec((rb, d), lambda j, i, be, nu: (jnp.minimum(i, nu[0] - 1), 0)),
                pl.BlockSpec((None, None, d, tf), lambda j, i, be, nu: (layer, be[i], 0, j)),
                pl.BlockSpec((None, None, d, tf), lambda j, i, be, nu: (layer, be[i], 0, j)),
            ],
            out_specs=pl.BlockSpec((rb, tf), lambda j, i, be, nu: (i, j)),
        ),
        out_shape=jax.ShapeDtypeStruct((rows, f), h_dtype),
        compiler_params=_cparams(("arbitrary", "arbitrary")),
        name="moe_up",
    )(blk_e, n_used, xs, w_gate, w_up)
    return pl.pallas_call(
        functools.partial(_moe_down_kernel, prec=prec),
        grid_spec=pltpu.PrefetchScalarGridSpec(
            num_scalar_prefetch=2,
            grid=(d // tn, nb),
            in_specs=[
                pl.BlockSpec((rb, f), lambda j, i, be, nu: (i, 0)),
                pl.BlockSpec((None, None, f, tn), lambda j, i, be, nu: (layer, be[i], 0, j)),
            ],
            out_specs=pl.BlockSpec((rb, tn), lambda j, i, be, nu: (i, j)),
        ),
        out_shape=jax.ShapeDtypeStruct((rows, d), F32),
        compiler_params=_cparams(("arbitrary", "arbitrary")),
        name="moe_down",
    )(blk_e, n_used, hmid, w_down)


def _combine_kernel(pos_ref, x_ref, route_ref, y_hbm, o_ref, buf, sem):
    tm = x_ref.shape[0]
    step = pl.program_id(0)
    slot = step % 2

    def dmas(tile, s, k):
        return _tile_row_dmas(y_hbm, buf.at[s, k], sem.at[s], tm,
                              lambda i: pos_ref[TOP_K * (tile * tm + i) + k])

    @pl.when(step == 0)
    def _():
        for k in range(TOP_K):
            dmas(0, 0, k)[0]()

    @pl.when(step + 1 < pl.num_programs(0))
    def _():
        for k in range(TOP_K):
            dmas(step + 1, 1 - slot, k)[0]()

    for k in range(TOP_K):
        dmas(step, slot, k)[1]()
    route = route_ref[...]
    g1 = route[:, TOP_K:TOP_K + 1]
    g2 = route[:, TOP_K + 1:TOP_K + 2]
    o_ref[...] = x_ref[...] + (buf[slot, 0] * g1 + buf[slot, 1] * g2)


def moe_combine(x, y, route, pos, *, tm):
    m, d = x.shape
    return pl.pallas_call(
        _combine_kernel,
        grid_spec=pltpu.PrefetchScalarGridSpec(
            num_scalar_prefetch=1,
            grid=(m // tm,),
            in_specs=[pl.BlockSpec((tm, d), lambda i, pos: (i, 0)),
                      pl.BlockSpec((tm, LANES), lambda i, pos: (i, 0)),
                      pl.BlockSpec(memory_space=pl.ANY)],
            out_specs=pl.BlockSpec((tm, d), lambda i, pos: (i, 0)),
            scratch_shapes=[pltpu.VMEM((2, TOP_K, tm, d), F32), pltpu.SemaphoreType.DMA((2,))],
        ),
        out_shape=jax.ShapeDtypeStruct((m, d), F32),
        compiler_params=_cparams(("arbitrary",)),
        name="moe_combine",
    )(pos, x, route, y)


def _dispatch_plan(route, counts, rb):
    n = route.shape[0]
    experts = jnp.arange(N_EXPERTS, dtype=jnp.int32)
    eid = route[:, :TOP_K].astype(jnp.int32)
    rank = route[:, 2 * TOP_K:3 * TOP_K].astype(jnp.int32)
    counts = counts.astype(jnp.int32)
    padded = (counts + rb - 1) // rb * rb
    pend = jnp.cumsum(padded)
    pstarts = pend - padded
    npair = n * TOP_K
    nb = (npair + N_EXPERTS * (rb - 1) + rb - 1) // rb
    blk_start = jnp.arange(nb, dtype=jnp.int32) * rb
    blk_e = jnp.minimum(jnp.sum(pend[None, :] <= blk_start[:, None], axis=1), N_EXPERTS - 1)
    start_of = jnp.sum(jnp.where(eid[..., None] == experts, pstarts, 0), axis=-1)
    pos = (start_of + rank).reshape(npair).astype(jnp.int32)
    n_used = (pend[-1:] // rb).astype(jnp.int32)
    return pos, blk_e.astype(jnp.int32), n_used, nb * rb


def moe_layer(x, g, wr, w_gate, w_up, w_down, layer, *, prec, rb, tf, tn, tm_router, tm_combine):
    h2, route, cnt = router(x, g, wr, layer, tm=tm_router)
    counts = cnt[0, N_GROUPS:N_GROUPS + N_EXPERTS]
    pos, blk_e, n_used, rows = _dispatch_plan(route, counts, rb)
    xs = gather_rows(h2, row_source(pos, rows=rows), rb=rb)
    h_dtype = BF16 if prec == 'bf16' else F32
    y = moe_experts(xs, blk_e, n_used, w_gate, w_up, w_down, layer,
                    rb=rb, tf=tf, tn=tn, prec=prec, h_dtype=h_dtype)
    return moe_combine(x, y, route, pos, tm=tm_combine)


GLA_COLS = 2 * GLA_HEADS * GLA_DK + 2 * GLA_HEADS * GLA_DV
ATT_COLS = 3 * 3 * HEADS_PER_GROUP * HEAD_DIM


PREC = 'bf16'
MOE_ROW_BLOCK = 128
BF16_ROWS = 16


def _row_tile(n, target):
    best = None
    for t in range(BF16_ROWS, target + 1, BF16_ROWS):
        if n % t == 0:
            best = t
    assert best is not None, (n, target)
    return best


def kernel(x_prompt, x_sample, cache_kv_w128, cache_kv_w512, cache_kv_w2048, state_gla,
           g_norm_mix, w_in, w_gla_gate_up, b_gla_gate, g_q, g_k, g_gla_out,
           w_gla_proj, w_att_proj, w_merge_gate, w_out, g_norm_ffn,
           w_router_group, w_router_expert, w_exp_gate, w_exp_up, w_exp_down):
    caches = (cache_kv_w128, cache_kv_w512, cache_kv_w2048)
    bp, sp, d = x_prompt.shape
    bs, ss, _ = x_sample.shape
    depth = w_in.shape[0]
    hpg = HEADS_PER_GROUP
    n_p, n_s = bp * sp, bs * ss
    n = n_p + n_s
    x = jnp.concatenate([x_prompt.reshape(n_p, d), x_sample.reshape(n_s, d)], axis=0)
    tm_norm = _row_tile(n, 256)
    tm_mm = _row_tile(n, 1536)
    tm_merge = _row_tile(n, 768)
    tm_combine = _row_tile(n, 128)

    glr0 = GLA_COLS
    att0 = GLA_COLS + GLA_RANK
    wup_pad = jnp.pad(w_gla_gate_up, ((0, 0), (0, LANES - GLA_RANK), (0, 0)))
    wr = jnp.pad(jnp.concatenate([w_router_group, w_router_expert], axis=-1),
                 ((0, 0), (0, 0), (0, LANES - N_GROUPS - N_EXPERTS)))
    zero_state = jnp.zeros((bp, GLA_HEADS, GLA_DK, GLA_DV), F32)
    w_in_t = jnp.swapaxes(w_in, 1, 2)

    kv_p = [[] for _ in range(3)]
    kv_s = [[] for _ in range(3)]
    st_p, st_s = [], []
    for l in range(depth):
        h = rmsnorm(x, g_norm_mix, l, out_dtype=BF16, tm=tm_norm)
        pg = matmul(h, w_in_t, prec=PREC, out_dtype=F32, tm=tm_mm, tn=256, layer=l, ncols=GLA_COLS,
                    w_is_transposed=True)
        pglr = matmul(h, w_in_t, prec=PREC, out_dtype=F32, tm=tm_mm, tn=LANES, layer=l, col0=glr0,
                      ncols=LANES, w_is_transposed=True)
        pa = matmul(h, w_in_t, prec=PREC, out_dtype=F32, tm=tm_mm, tn=256, layer=l, col0=att0,
                    ncols=ATT_COLS, w_is_transposed=True)

        og_p, st = gla(pg, pglr, wup_pad, b_gla_gate, g_gla_out, zero_state, l, batch=bp, seq=sp,
                       chunk=64, rows_per_step=512, row0=0, prec=PREC, out_dtype=BF16)
        st_p.append(st)
        og_s, st = gla(pg, pglr, wup_pad, b_gla_gate, g_gla_out, state_gla, l, batch=bs, seq=ss,
                       chunk=ss, rows_per_step=ss, row0=n_p, prec=PREC, out_dtype=F32)
        st_s.append(st)

        att_p, kvs = attn_prompt(pa, g_q, g_k, l, batch=bp, seq=sp, prec=PREC, out_dtype=BF16)
        for g in range(3):
            w = min(WINDOWS[g], sp)
            kv_p[g].append(jnp.stack([kvs[g][0].reshape(bp, w, hpg, HEAD_DIM),
                                      kvs[g][1].reshape(bp, w, hpg, HEAD_DIM)], axis=2))
        att_s, kvs = attn_sample(pa, caches, g_q, g_k, l, batch=bs, n_new=ss, row0=n_p,
                                 prec=PREC, out_dtype=F32)
        for g in range(3):
            kv_s[g].append(jnp.stack([kvs[g][0].reshape(bs, ss, hpg, HEAD_DIM),
                                      kvs[g][1].reshape(bs, ss, hpg, HEAD_DIM)], axis=2))

        og = jnp.concatenate([og_p, og_s.astype(BF16)], axis=0)
        att = jnp.concatenate([att_p, att_s.astype(BF16)], axis=0)
        mixed = merge(h, og, att, w_merge_gate, w_gla_proj, w_att_proj, l, prec=PREC, out_dtype=BF16,
                      tm=tm_merge, tn=256)
        x = matmul(mixed, w_out, prec=PREC, out_dtype=F32, tm=tm_mm, tn=256, layer=l, residual=x)
        x = moe_layer(x, g_norm_ffn, wr, w_exp_gate, w_exp_up, w_exp_down, l, prec=PREC,
                      rb=MOE_ROW_BLOCK, tf=512, tn=2048, tm_router=tm_norm, tm_combine=tm_combine)

    xp, xs = x[:n_p], x[n_p:]
    return (xp.reshape(bp, sp, d), xs.reshape(bs, ss, d),
            jnp.stack(kv_p[0]), jnp.stack(kv_s[0]),
            jnp.stack(kv_p[1]), jnp.stack(kv_s[1]),
            jnp.stack(kv_p[2]), jnp.stack(kv_s[2]),
            jnp.stack(st_p), jnp.stack(st_s))
```
